```python
import jax, jax.numpy as jnp
from jax import lax
import numpy as np

D_MODEL = 2048
BATCH = 2
SEQ = 16384
DEPTH = 4
DEC_BATCH = 1
DEC_SEQ = 16384
PAST_LEN = 128

N_MIXERS = 2
N_ATTN_LAYERS = (DEPTH + N_MIXERS - 1) // N_MIXERS
N_CONV_LAYERS = DEPTH // N_MIXERS
HEAD_DIM = 128
DILATION_GROUPS = ((128, 1), (512, 4), (2048, 16))
N_GROUPS = len(DILATION_GROUPS)
HEADS_PER_GROUP = 4
N_HEADS = N_GROUPS * HEADS_PER_GROUP
ATTN_WIDTH = N_HEADS * HEAD_DIM
ROT_DIM = HEAD_DIM // 4
ROPE_THETA = 500000.0
CONV_CHANNELS = D_MODEL
CONV_WIDTH = 31
D_FF = 5632
FFN_RESIDUAL_WEIGHT = 0.5
RMS_EPS = 1e-6
LN_EPS = 1e-5
NEG_INF = -1e30

kernel_name = "hybrid_dilated_attn_conformer_conv_encoder"


def _rms_norm(x, g):
    x32 = x.astype(jnp.float32)
    y = x32 * lax.rsqrt(jnp.mean(x32 * x32, axis=-1, keepdims=True) + RMS_EPS)
    return (y * g.astype(jnp.float32)).astype(x.dtype)


def _swiglu(x, w_gate, w_up, w_down):
    return (jax.nn.silu(x @ w_gate) * (x @ w_up)) @ w_down


def _partial_rope(t, seq_len):
    pos = jnp.arange(seq_len, dtype=jnp.float32)
    freqs = ROPE_THETA ** (-jnp.arange(0, ROT_DIM, 2, dtype=jnp.float32) / ROT_DIM)
    ang = pos[:, None] * freqs[None, :]
    cos = jnp.cos(ang)[None, :, None, :]
    sin = jnp.sin(ang)[None, :, None, :]
    t32 = t.astype(jnp.float32)
    half = ROT_DIM // 2
    t1, t2, rest = t32[..., :half], t32[..., half:ROT_DIM], t32[..., ROT_DIM:]
    out = jnp.concatenate([t1 * cos - t2 * sin, t2 * cos + t1 * sin, rest], axis=-1)
    return out.astype(t.dtype)


def _local_attention(q, k, v, half):
    N, L, H, Dh = q.shape
    nb = -(-L // half)
    Lp = nb * half
    pad = Lp - L
    qb = jnp.pad(q, ((0, 0), (0, pad), (0, 0), (0, 0))).reshape(N, nb, half, H, Dh).astype(jnp.float32)

    def key_blocks(t):
        tp = jnp.pad(t, ((0, 0), (half, pad + half), (0, 0), (0, 0))).reshape(N, nb + 2, half, H, Dh)
        return jnp.concatenate([tp[:, :-2], tp[:, 1:-1], tp[:, 2:]], axis=2).astype(jnp.float32)

    kb, vb = key_blocks(k), key_blocks(v)
    scores = jnp.einsum('nbqhd,nbkhd->nbhqk', qb, kb) * (Dh ** -0.5)
    blk = jnp.arange(nb, dtype=jnp.int32)[:, None] * half
    qpos = blk + jnp.arange(half, dtype=jnp.int32)[None, :]
    kpos = blk - half + jnp.arange(3 * half, dtype=jnp.int32)[None, :]
    valid = ((jnp.abs(qpos[:, :, None] - kpos[:, None, :]) <= half)
             & (kpos[:, None, :] >= 0) & (kpos[:, None, :] < L))
    scores = jnp.where(valid[None, :, None], scores, NEG_INF)
    m = jnp.max(scores, axis=-1, keepdims=True)
    e = jnp.exp(scores - m)
    den = jnp.sum(e, axis=-1, keepdims=True)
    out = jnp.einsum('nbhqk,nbkhd->nbqhd', e / den, vb)
    lse = (m + jnp.log(den))[..., 0]
    out = out.reshape(N, Lp, H, Dh)[:, :L]
    lse = lse.transpose(0, 1, 3, 2).reshape(N, Lp, H)[:, :L]
    return out, lse


def _dilated_attention(q, k, v, dilation, half):
    B, S, H, Dh = q.shape
    L = S // dilation

    def to_residue(t):
        return t.reshape(B, L, dilation, H, Dh).transpose(0, 2, 1, 3, 4).reshape(B * dilation, L, H, Dh)

    out, lse = _local_attention(to_residue(q), to_residue(k), to_residue(v), half)
    out = out.reshape(B, dilation, L, H, Dh).transpose(0, 2, 1, 3, 4).reshape(B, S, H, Dh)
    lse = lse.reshape(B, dilation, L, H).transpose(0, 2, 1, 3).reshape(B, S, H)
    return out, lse


def _dilated_mixture_attention(x, w_qkv, w_o):
    B, S, _ = x.shape
    qkv = (x @ w_qkv).reshape(B, S, 3, N_HEADS, HEAD_DIM)
    q = _partial_rope(qkv[:, :, 0], S)
    k = _partial_rope(qkv[:, :, 1], S)
    v = qkv[:, :, 2]
    outs, lses = [], []
    for g, (window, dilation) in enumerate(DILATION_GROUPS):
        sl = slice(g * HEADS_PER_GROUP, (g + 1) * HEADS_PER_GROUP)
        o, l = _dilated_attention(q[:, :, sl], k[:, :, sl], v[:, :, sl], dilation, window // (2 * dilation))
        outs.append(o)
        lses.append(l)
    o = jnp.stack(outs, axis=2)
    lam = jax.nn.softmax(jnp.stack(lses, axis=2), axis=2)
    o = (o * lam[..., None]).reshape(B, S, ATTN_WIDTH).astype(x.dtype)
    return o @ w_o


def _conformer_conv(x, w_in, b_in, w_dw, b_dw, ln_g, ln_b, w_out, b_out):
    h = x @ w_in + b_in
    h = h[..., :CONV_CHANNELS] * jax.nn.sigmoid(h[..., CONV_CHANNELS:])
    h = lax.conv_general_dilated(
        h, w_dw[:, None, :].astype(h.dtype), window_strides=(1,),
        padding=[(CONV_WIDTH // 2, CONV_WIDTH // 2)],
        dimension_numbers=('NWC', 'WIO', 'NWC'),
        feature_group_count=CONV_CHANNELS) + b_dw
    h32 = h.astype(jnp.float32)
    mu = jnp.mean(h32, axis=-1, keepdims=True)
    var = jnp.mean(jnp.square(h32 - mu), axis=-1, keepdims=True)
    h32 = (h32 - mu) * lax.rsqrt(var + LN_EPS) * ln_g.astype(jnp.float32) + ln_b.astype(jnp.float32)
    h = jax.nn.silu(h32).astype(x.dtype)
    return h @ w_out + b_out


def _trunk(x, ffn_norm, ffn_w_gate, ffn_w_up, ffn_w_down, mix_norm,
           attn_w_qkv, attn_w_o, conv_w_in, conv_b_in, conv_w_dw, conv_b_dw,
           conv_ln_g, conv_ln_b, conv_w_out, conv_b_out, final_norm):
    for i in range(DEPTH):
        x = x + FFN_RESIDUAL_WEIGHT * _swiglu(_rms_norm(x, ffn_norm[i, 0]), ffn_w_gate[i, 0], ffn_w_up[i, 0], ffn_w_down[i, 0])
        h = _rms_norm(x, mix_norm[i])
        j = i // N_MIXERS
        if i % N_MIXERS == 0:
            h = _dilated_mixture_attention(h, attn_w_qkv[j], attn_w_o[j])
        else:
            h = _conformer_conv(h, conv_w_in[j], conv_b_in[j], conv_w_dw[j], conv_b_dw[j],
                                conv_ln_g[j], conv_ln_b[j], conv_w_out[j], conv_b_out[j])
        x = x + h
        x = x + FFN_RESIDUAL_WEIGHT * _swiglu(_rms_norm(x, ffn_norm[i, 1]), ffn_w_gate[i, 1], ffn_w_up[i, 1], ffn_w_down[i, 1])
    return _rms_norm(x, final_norm)


def setup_inputs(seed: int = 0) -> dict:
    key = jax.random.key(seed)
    ks = jax.random.split(key, 20)
    f32 = jnp.float32

    def nrm(k, shape, scale):
        return jax.random.normal(k, shape, f32) * scale

    return {
        "x_prompt": nrm(ks[0], (BATCH, SEQ, D_MODEL), 1.0),
        "x_sample": nrm(ks[1], (DEC_BATCH, DEC_SEQ, D_MODEL), 1.0),
        "ffn_norm": 1.0 + nrm(ks[2], (DEPTH, 2, D_MODEL), 0.02),
        "ffn_w_gate": nrm(ks[3], (DEPTH, 2, D_MODEL, D_FF), D_MODEL ** -0.5),
        "ffn_w_up": nrm(ks[4], (DEPTH, 2, D_MODEL, D_FF), D_MODEL ** -0.5),
        "ffn_w_down": nrm(ks[5], (DEPTH, 2, D_FF, D_MODEL), D_FF ** -0.5),
        "mix_norm": 1.0 + nrm(ks[6], (DEPTH, D_MODEL), 0.02),
        "attn_w_qkv": nrm(ks[7], (N_ATTN_LAYERS, D_MODEL, 3 * ATTN_WIDTH), D_MODEL ** -0.5),
        "attn_w_o": nrm(ks[8], (N_ATTN_LAYERS, ATTN_WIDTH, D_MODEL), ATTN_WIDTH ** -0.5),
        "conv_w_in": nrm(ks[9], (N_CONV_LAYERS, D_MODEL, 2 * CONV_CHANNELS), D_MODEL ** -0.5),
        "conv_b_in": nrm(ks[10], (N_CONV_LAYERS, 2 * CONV_CHANNELS), 0.02),
        "conv_w_dw": nrm(ks[11], (N_CONV_LAYERS, CONV_WIDTH, CONV_CHANNELS), CONV_WIDTH ** -0.5),
        "conv_b_dw": nrm(ks[12], (N_CONV_LAYERS, CONV_CHANNELS), 0.02),
        "conv_ln_g": 1.0 + nrm(ks[13], (N_CONV_LAYERS, CONV_CHANNELS), 0.02),
        "conv_ln_b": nrm(ks[14], (N_CONV_LAYERS, CONV_CHANNELS), 0.02),
        "conv_w_out": nrm(ks[15], (N_CONV_LAYERS, CONV_CHANNELS, D_MODEL), CONV_CHANNELS ** -0.5),
        "conv_b_out": nrm(ks[16], (N_CONV_LAYERS, D_MODEL), 0.02),
        "final_norm": 1.0 + nrm(ks[17], (D_MODEL,), 0.02),
    }


def reference(x_prompt, x_sample, ffn_norm, ffn_w_gate, ffn_w_up, ffn_w_down, mix_norm,
              attn_w_qkv, attn_w_o, conv_w_in, conv_b_in, conv_w_dw, conv_b_dw,
              conv_ln_g, conv_ln_b, conv_w_out, conv_b_out, final_norm):
    y_prompt = _trunk(x_prompt, ffn_norm, ffn_w_gate, ffn_w_up, ffn_w_down, mix_norm,
                      attn_w_qkv, attn_w_o, conv_w_in, conv_b_in, conv_w_dw, conv_b_dw,
                      conv_ln_g, conv_ln_b, conv_w_out, conv_b_out, final_norm)
    y_sample = _trunk(x_sample, ffn_norm, ffn_w_gate, ffn_w_up, ffn_w_down, mix_norm,
                      attn_w_qkv, attn_w_o, conv_w_in, conv_b_in, conv_w_dw, conv_b_dw,
                      conv_ln_g, conv_ln_b, conv_w_out, conv_b_out, final_norm)
    return (y_prompt, y_sample)
```

```python
import functools

import jax
import jax.numpy as jnp
from jax import lax
from jax.experimental import pallas as pl
from jax.experimental.pallas import tpu as pltpu

F32 = jnp.float32
BF16 = jnp.bfloat16

HEAD_DIM = 128
ROT_DIM = HEAD_DIM // 4
ROPE_THETA = 500000.0
DILATION_GROUPS = ((128, 1), (512, 4), (2048, 16))
HEADS_PER_GROUP = 4
GROUP_WIDTH = HEADS_PER_GROUP * HEAD_DIM
N_GROUPS = len(DILATION_GROUPS)
ATTN_WIDTH = N_GROUPS * GROUP_WIDTH
CONV_WIDTH = 31
CONV_PAD = CONV_WIDTH // 2
FFN_RESIDUAL_WEIGHT = 0.5
RMS_EPS = 1e-6
LN_EPS = 1e-5
NEG_INF = -1e30

LANES = 128
VMEM_LIMIT_BYTES = 56 * 1024 * 1024

FFN_ROWS = 768
FFN_COLS = 512
MIX_ROWS = 512
ATTN_ROWS = 512
CONV_HALO = 16
CONV_CHUNK = 64


def _params(n_grid):
    return pltpu.CompilerParams(
        dimension_semantics=("arbitrary",) * n_grid,
        vmem_limit_bytes=VMEM_LIMIT_BYTES)


def _resident(shape, index_map):
    return pl.BlockSpec(shape, index_map, pipeline_mode=pl.Buffered(1))


def _rms(x, g):
    y = x * lax.rsqrt(jnp.mean(x * x, axis=-1, keepdims=True) + RMS_EPS)
    return y * g


def _sigmoid(x):
    return 1.0 / (1.0 + jnp.exp(-x))


def _ffn_kernel(x_ref, g_ref, wg_ref, wu_ref, wd_ref, fg_ref, o_ref, xn_ref, *, final_norm):
    f = pl.program_id(1)

    @pl.when(f == 0)
    def _():
        x = x_ref[...]
        xn_ref[...] = _rms(x, g_ref[...]).astype(BF16)
        o_ref[...] = x

    xn = xn_ref[...]
    gate = jnp.dot(xn, wg_ref[...], preferred_element_type=F32)
    up = jnp.dot(xn, wu_ref[...], preferred_element_type=F32)
    h = (gate * _sigmoid(gate) * up * FFN_RESIDUAL_WEIGHT).astype(BF16)
    o_ref[...] += jnp.dot(h, wd_ref[...], preferred_element_type=F32)

    if final_norm:
        @pl.when(f == pl.num_programs(1) - 1)
        def _():
            o_ref[...] = _rms(o_ref[...], fg_ref[...])


def _ffn(x, norm_g, w_gate, w_up, w_down, layer, which, final_g=None):
    T, D = x.shape
    F = w_gate.shape[-1]
    tm = min(FFN_ROWS, T)
    tf = min(FFN_COLS, F)
    assert T % tm == 0 and F % tf == 0
    final_norm = final_g is not None
    if final_g is None:
        final_g = norm_g[layer, which]
    kernel = functools.partial(_ffn_kernel, final_norm=final_norm)
    return pl.pallas_call(
        kernel,
        out_shape=jax.ShapeDtypeStruct((T, D), F32),
        grid=(T // tm, F // tf),
        in_specs=[
            pl.BlockSpec((tm, D), lambda i, f: (i, 0)),
            pl.BlockSpec((None, None, 1, D), lambda i, f: (layer, which, 0, 0)),
            pl.BlockSpec((None, None, D, tf), lambda i, f: (layer, which, 0, f)),
            pl.BlockSpec((None, None, D, tf), lambda i, f: (layer, which, 0, f)),
            pl.BlockSpec((None, None, tf, D), lambda i, f: (layer, which, f, 0)),
            pl.BlockSpec((1, D), lambda i, f: (0, 0)),
        ],
        out_specs=pl.BlockSpec((tm, D), lambda i, f: (i, 0)),
        scratch_shapes=[pltpu.VMEM((tm, D), BF16)],
        compiler_params=_params(2),
        name="ffn",
    )(x, norm_g.reshape(norm_g.shape[0], 2, 1, D), w_gate, w_up, w_down,
      final_g.reshape(1, D))


def _rope_tables(seq_len):
    half = ROT_DIM // 2
    pos = jnp.arange(seq_len, dtype=F32)
    freqs = ROPE_THETA ** (-jnp.arange(0, ROT_DIM, 2, dtype=F32) / ROT_DIM)
    ang = pos[:, None] * freqs[None, :]
    cos, sin = jnp.cos(ang), jnp.sin(ang)
    zeros = jnp.zeros((seq_len, HEAD_DIM - ROT_DIM), F32)
    zhalf = jnp.zeros((seq_len, half), F32)
    c = jnp.concatenate([cos, cos, jnp.ones_like(zeros)], axis=-1)
    s_lo = jnp.concatenate([-sin, zhalf, zeros], axis=-1)
    s_hi = jnp.concatenate([zhalf, sin, zeros], axis=-1)
    return c, s_lo, s_hi


def _qkv_kernel(x_ref, g_ref, w_ref, c_ref, slo_ref, shi_ref, o0_ref, o1_ref, o2_ref,
                xn_ref, t_ref, *, tm):
    xn_ref[...] = _rms(x_ref[...], g_ref[...]).astype(BF16)
    xn = xn_ref[...]
    c, s_lo, s_hi = c_ref[...], slo_ref[...], shi_ref[...]
    half = ROT_DIM // 2
    scale = HEAD_DIM ** -0.5
    o_refs = (o0_ref, o1_ref, o2_ref)
    for part in range(3):
        for g, (_, dil) in enumerate(DILATION_GROUPS):
            col = part * ATTN_WIDTH + g * GROUP_WIDTH
            acc = jnp.dot(xn, w_ref[:, col:col + GROUP_WIDTH], preferred_element_type=F32)
            dst = o_refs[g]
            for j in range(HEADS_PER_GROUP):
                t = acc[:, j * HEAD_DIM:(j + 1) * HEAD_DIM]
                if part < 2:
                    t = (t * c + pltpu.roll(t, HEAD_DIM - half, 1) * s_lo
                         + pltpu.roll(t, half, 1) * s_hi)
                if part == 0:
                    t = t * scale
                cols = slice(part * GROUP_WIDTH + j * HEAD_DIM,
                             part * GROUP_WIDTH + (j + 1) * HEAD_DIM)
                if dil == 1:
                    dst[0, :, cols] = t.astype(BF16)
                else:
                    t_ref[j] = t
                    for r in range(dil):
                        dst[r, :, cols] = t_ref[j, pl.ds(r, tm // dil, stride=dil), :].astype(BF16)


def _qkv(x, norm_g, w_qkv, tables, n_seq, seq_len):
    T, D = x.shape
    tm = min(MIX_ROWS, seq_len)
    assert seq_len % tm == 0
    W = w_qkv.shape[-1]
    x3 = x.reshape(n_seq, seq_len, D)
    out_shapes, out_specs = [], []
    for _, dil in DILATION_GROUPS:
        assert tm % (dil * 16) == 0
        out_shapes.append(jax.ShapeDtypeStruct((n_seq, dil, seq_len // dil, 3 * GROUP_WIDTH), BF16))
        out_specs.append(pl.BlockSpec((None, dil, tm // dil, 3 * GROUP_WIDTH),
                                      lambda b, i: (b, 0, i, 0)))
    tab_spec = pl.BlockSpec((tm, HEAD_DIM), lambda b, i: (i, 0))
    return pl.pallas_call(
        functools.partial(_qkv_kernel, tm=tm),
        out_shape=out_shapes,
        grid=(n_seq, seq_len // tm),
        in_specs=[
            pl.BlockSpec((None, tm, D), lambda b, i: (b, i, 0)),
            pl.BlockSpec((1, D), lambda b, i: (0, 0)),
            _resident((D, W), lambda b, i: (0, 0)),
            tab_spec, tab_spec, tab_spec,
        ],
        out_specs=out_specs,
        scratch_shapes=[pltpu.VMEM((tm, D), BF16),
                        pltpu.VMEM((HEADS_PER_GROUP, tm, HEAD_DIM), F32)],
        compiler_params=_params(2),
        name="qkv_rope",
    )(x3, norm_g.reshape(1, D), w_qkv, *tables)


def _attn_kernel(q_ref, kp_ref, kc_ref, kn_ref, vp_ref, vc_ref, vn_ref, o_ref, lse_ref,
                 kbuf, vbuf, *, tq, half, sub_len):
    i = pl.program_id(2)
    kbuf[0:half] = kp_ref[...]
    kbuf[half:half + tq] = kc_ref[...]
    kbuf[half + tq:] = kn_ref[...]
    vbuf[0:half] = vp_ref[...]
    vbuf[half:half + tq] = vc_ref[...]
    vbuf[half + tq:] = vn_ref[...]

    sub = 2 * half
    win = sub + 2 * half
    row = lax.broadcasted_iota(jnp.int32, (sub, win), 0)
    col = lax.broadcasted_iota(jnp.int32, (sub, win), 1)
    band = (col - row).astype(jnp.uint32) <= jnp.uint32(2 * half)
    lane = lax.broadcasted_iota(jnp.int32, (sub, LANES), 1)
    for s in range(tq // sub):
        kpos = i * tq + (s * sub - half) + col
        inside = kpos.astype(jnp.uint32) < jnp.uint32(sub_len)
        bias = jnp.where(band, jnp.where(inside, 0.0, NEG_INF), NEG_INF).astype(F32)
        lse_tile = jnp.zeros((sub, LANES), F32)
        for j in range(HEADS_PER_GROUP):
            hs = slice(j * HEAD_DIM, (j + 1) * HEAD_DIM)
            q = q_ref[s * sub:(s + 1) * sub, hs]
            k = kbuf[s * sub:s * sub + win, hs]
            v = vbuf[s * sub:s * sub + win, hs]
            sc = lax.dot_general(q, k, (((1,), (1,)), ((), ())), preferred_element_type=F32)
            sc = sc + bias
            m = jnp.max(sc, axis=-1, keepdims=True)
            e = jnp.exp(sc - m)
            den = jnp.sum(e, axis=-1, keepdims=True)
            o = jnp.dot(e.astype(BF16), v, preferred_element_type=F32) / den
            o_ref[s * sub:(s + 1) * sub, hs] = o.astype(BF16)
            lse_tile = jnp.where(lane == j, m + jnp.log(den), lse_tile)
        lse_ref[s * sub:(s + 1) * sub, :] = lse_tile


def _attention(qkv_g, window, dil):
    n_seq, _, L, _ = qkv_g.shape
    half = window // (2 * dil)
    tq = min(ATTN_ROWS, L)
    assert L % tq == 0 and tq % (2 * half) == 0 and half % 16 == 0
    per = tq // half
    last = L // half - 1

    def cur(c):
        return pl.BlockSpec((None, None, tq, GROUP_WIDTH), lambda b, r, i: (b, r, i, c))

    def prev(c):
        return pl.BlockSpec((None, None, half, GROUP_WIDTH),
                            lambda b, r, i: (b, r, jnp.maximum(i * per - 1, 0), c))

    def nxt(c):
        return pl.BlockSpec((None, None, half, GROUP_WIDTH),
                            lambda b, r, i: (b, r, jnp.minimum((i + 1) * per, last), c))

    kernel = functools.partial(_attn_kernel, tq=tq, half=half, sub_len=L)
    return pl.pallas_call(
        kernel,
        out_shape=[jax.ShapeDtypeStruct((n_seq, dil, L, GROUP_WIDTH), BF16),
                   jax.ShapeDtypeStruct((n_seq, dil, L, LANES), F32)],
        grid=(n_seq, dil, L // tq),
        in_specs=[cur(0), prev(1), cur(1), nxt(1), prev(2), cur(2), nxt(2)],
        out_specs=[pl.BlockSpec((None, None, tq, GROUP_WIDTH), lambda b, r, i: (b, r, i, 0)),
                   pl.BlockSpec((None, None, tq, LANES), lambda b, r, i: (b, r, i, 0))],
        scratch_shapes=[pltpu.VMEM((tq + 2 * half, GROUP_WIDTH), BF16),
                        pltpu.VMEM((tq + 2 * half, GROUP_WIDTH), BF16)],
        compiler_params=_params(3),
        name=f"band_attn_d{dil}",
    )(qkv_g, qkv_g, qkv_g, qkv_g, qkv_g, qkv_g, qkv_g)


def _wo_kernel(x_ref, o0_ref, o1_ref, o2_ref, l0_ref, l1_ref, l2_ref, w_ref, out_ref,
               obuf, lbuf, *, tm):
    o_refs = (o0_ref, o1_ref, o2_ref)
    l_refs = (l0_ref, l1_ref, l2_ref)
    for g, (_, dil) in enumerate(DILATION_GROUPS):
        for r in range(dil):
            rows = pl.ds(r, tm // dil, stride=dil) if dil > 1 else slice(None)
            for j in range(HEADS_PER_GROUP):
                hs = slice(j * HEAD_DIM, (j + 1) * HEAD_DIM)
                obuf[g * HEADS_PER_GROUP + j, rows, :] = o_refs[g][r, :, hs].astype(F32)
            lbuf[g, rows, :] = l_refs[g][r]
    lse = [lbuf[g] for g in range(N_GROUPS)]
    m = jnp.maximum(jnp.maximum(lse[0], lse[1]), lse[2])
    e = [jnp.exp(l - m) for l in lse]
    inv = 1.0 / (e[0] + e[1] + e[2])
    acc = x_ref[...]
    for g in range(N_GROUPS):
        lam = e[g] * inv
        heads = [obuf[g * HEADS_PER_GROUP + j] * lam[:, j:j + 1]
                 for j in range(HEADS_PER_GROUP)]
        a = jnp.concatenate(heads, axis=-1).astype(BF16)
        acc = acc + jnp.dot(a, w_ref[g * GROUP_WIDTH:(g + 1) * GROUP_WIDTH, :],
                            preferred_element_type=F32)
    out_ref[...] = acc


def _wo(x, outs, lses, w_o, n_seq, seq_len):
    T, D = x.shape
    tm = min(MIX_ROWS, seq_len)
    x3 = x.reshape(n_seq, seq_len, D)
    o_specs, l_specs = [], []
    for _, dil in DILATION_GROUPS:
        o_specs.append(pl.BlockSpec((None, dil, tm // dil, GROUP_WIDTH), lambda b, i: (b, 0, i, 0)))
        l_specs.append(pl.BlockSpec((None, dil, tm // dil, LANES), lambda b, i: (b, 0, i, 0)))
    out = pl.pallas_call(
        functools.partial(_wo_kernel, tm=tm),
        out_shape=jax.ShapeDtypeStruct((n_seq, seq_len, D), F32),
        grid=(n_seq, seq_len // tm),
        in_specs=[pl.BlockSpec((None, tm, D), lambda b, i: (b, i, 0)), *o_specs, *l_specs,
                  _resident((ATTN_WIDTH, D), lambda b, i: (0, 0))],
        out_specs=pl.BlockSpec((None, tm, D), lambda b, i: (b, i, 0)),
        scratch_shapes=[pltpu.VMEM((N_GROUPS * HEADS_PER_GROUP, tm, HEAD_DIM), F32),
                        pltpu.VMEM((N_GROUPS, tm, LANES), F32)],
        compiler_params=_params(2),
        name="attn_out_proj",
    )(x3, *outs, *lses, w_o)
    return out.reshape(T, D)


def _glu_kernel(x_ref, g_ref, w_ref, b_ref, u_ref, xn_ref, *, n_ch, tn):
    xn_ref[...] = _rms(x_ref[...], g_ref[...]).astype(BF16)
    xn = xn_ref[...]
    for c in range(0, n_ch, tn):
        a = jnp.dot(xn, w_ref[:, c:c + tn], preferred_element_type=F32) + b_ref[:, c:c + tn]
        gt = (jnp.dot(xn, w_ref[:, n_ch + c:n_ch + c + tn], preferred_element_type=F32)
              + b_ref[:, n_ch + c:n_ch + c + tn])
        u_ref[:, c:c + tn] = a * _sigmoid(gt)


def _glu(x, norm_g, w_in, b_in):
    T, D = x.shape
    n_ch = w_in.shape[-1] // 2
    tm = min(MIX_ROWS, T)
    tn = min(512, n_ch)
    assert T % tm == 0 and n_ch % tn == 0
    return pl.pallas_call(
        functools.partial(_glu_kernel, n_ch=n_ch, tn=tn),
        out_shape=jax.ShapeDtypeStruct((T, n_ch), F32),
        grid=(T // tm,),
        in_specs=[pl.BlockSpec((tm, D), lambda i: (i, 0)),
                  pl.BlockSpec((1, D), lambda i: (0, 0)),
                  _resident((D, 2 * n_ch), lambda i: (0, 0)),
                  pl.BlockSpec((1, 2 * n_ch), lambda i: (0, 0))],
        out_specs=pl.BlockSpec((tm, n_ch), lambda i: (i, 0)),
        scratch_shapes=[pltpu.VMEM((tm, D), BF16)],
        compiler_params=_params(1),
        name="conv_in_glu",
    )(x, norm_g.reshape(1, D), w_in, b_in.reshape(1, 2 * n_ch))


def _conv_out_kernel(x_ref, up_ref, uc_ref, un_ref, wdw_ref, bdw_ref, lng_ref, lnb_ref,
                     w_ref, bo_ref, out_ref, ubuf, cbuf, hbuf, *, tm, n_ch):
    i = pl.program_id(1)
    n_chunks = n_ch // LANES
    first = i == 0
    last = i == pl.num_programs(1) - 1
    for c in range(n_chunks):
        cs = slice(c * LANES, (c + 1) * LANES)
        ubuf[c, 0:CONV_HALO, :] = jnp.where(first, 0.0, up_ref[:, cs])
        ubuf[c, CONV_HALO:CONV_HALO + tm, :] = uc_ref[:, cs]
        ubuf[c, CONV_HALO + tm:, :] = jnp.where(last, 0.0, un_ref[:, cs])

    def conv_chunk(c, carry):
        w = wdw_ref[c]
        b = bdw_ref[c]
        for r0 in range(0, tm, CONV_CHUNK):
            acc = jnp.zeros((CONV_CHUNK, LANES), F32)
            for j in range(CONV_WIDTH):
                start = r0 + CONV_HALO - CONV_PAD + j
                acc = acc + ubuf[c, start:start + CONV_CHUNK, :] * w[j:j + 1, :]
            cbuf[c, r0:r0 + CONV_CHUNK, :] = acc + b
        return carry

    lax.fori_loop(0, n_chunks, conv_chunk, 0)

    tot = cbuf[0]
    for c in range(1, n_chunks):
        tot = tot + cbuf[c]
    mu = jnp.sum(tot, axis=-1, keepdims=True) * (1.0 / n_ch)
    sq = jnp.zeros((tm, LANES), F32)
    for c in range(n_chunks):
        d = cbuf[c] - mu
        sq = sq + d * d
    var = jnp.sum(sq, axis=-1, keepdims=True) * (1.0 / n_ch)
    rstd = lax.rsqrt(var + LN_EPS)
    for c in range(n_chunks):
        cs = slice(c * LANES, (c + 1) * LANES)
        y = (cbuf[c] - mu) * rstd * lng_ref[:, cs] + lnb_ref[:, cs]
        hbuf[:, cs] = (y * _sigmoid(y)).astype(BF16)

    out_ref[...] = (x_ref[...] + bo_ref[...]
                    + jnp.dot(hbuf[...], w_ref[...], preferred_element_type=F32))


def _conv_out(x, u, w_dw, b_dw, ln_g, ln_b, w_out, b_out, n_seq, seq_len):
    T, D = x.shape
    n_ch = u.shape[-1]
    tm = min(MIX_ROWS, seq_len)
    assert seq_len % tm == 0 and tm % CONV_HALO == 0 and tm % CONV_CHUNK == 0
    assert n_ch % LANES == 0 and CONV_HALO >= CONV_PAD
    n_chunks = n_ch // LANES
    per = tm // CONV_HALO
    last = seq_len // CONV_HALO - 1
    x3 = x.reshape(n_seq, seq_len, D)
    u3 = u.reshape(n_seq, seq_len, n_ch)
    w_dw_c = w_dw.reshape(CONV_WIDTH, n_chunks, LANES).transpose(1, 0, 2)
    b_dw_c = b_dw.reshape(n_chunks, 1, LANES)
    out = pl.pallas_call(
        functools.partial(_conv_out_kernel, tm=tm, n_ch=n_ch),
        out_shape=jax.ShapeDtypeStruct((n_seq, seq_len, D), F32),
        grid=(n_seq, seq_len // tm),
        in_specs=[
            pl.BlockSpec((None, tm, D), lambda b, i: (b, i, 0)),
            pl.BlockSpec((None, CONV_HALO, n_ch),
                         lambda b, i: (b, jnp.maximum(i * per - 1, 0), 0)),
            pl.BlockSpec((None, tm, n_ch), lambda b, i: (b, i, 0)),
            pl.BlockSpec((None, CONV_HALO, n_ch),
                         lambda b, i: (b, jnp.minimum((i + 1) * per, last), 0)),
            pl.BlockSpec((n_chunks, CONV_WIDTH, LANES), lambda b, i: (0, 0, 0)),
            pl.BlockSpec((n_chunks, 1, LANES), lambda b, i: (0, 0, 0)),
            pl.BlockSpec((1, n_ch), lambda b, i: (0, 0)),
            pl.BlockSpec((1, n_ch), lambda b, i: (0, 0)),
            _resident((n_ch, D), lambda b, i: (0, 0)),
            pl.BlockSpec((1, D), lambda b, i: (0, 0)),
        ],
        out_specs=pl.BlockSpec((None, tm, D), lambda b, i: (b, i, 0)),
        scratch_shapes=[pltpu.VMEM((n_chunks, tm + 2 * CONV_HALO, LANES), F32),
                        pltpu.VMEM((n_chunks, tm, LANES), F32),
                        pltpu.VMEM((tm, n_ch), BF16)],
        compiler_params=_params(2),
        name="conv_ln_out",
    )(x3, u3, u3, u3, w_dw_c, b_dw_c, ln_g.reshape(1, n_ch), ln_b.reshape(1, n_ch),
      w_out, b_out.reshape(1, D))
    return out.reshape(T, D)


def kernel(x_prompt, x_sample, ffn_norm, ffn_w_gate, ffn_w_up, ffn_w_down, mix_norm, attn_w_qkv, attn_w_o, conv_w_in, conv_b_in, conv_w_dw, conv_b_dw, conv_ln_g, conv_ln_b, conv_w_out, conv_b_out, final_norm):
    n_prompt, seq_len, D = x_prompt.shape
    n_sample = x_sample.shape[0]
    assert x_sample.shape[1:] == (seq_len, D)
    n_seq = n_prompt + n_sample
    depth = ffn_norm.shape[0]

    wg, wu, wd = (w.astype(BF16) for w in (ffn_w_gate, ffn_w_up, ffn_w_down))
    w_qkv, w_o = attn_w_qkv.astype(BF16), attn_w_o.astype(BF16)
    w_in, w_out = conv_w_in.astype(BF16), conv_w_out.astype(BF16)
    tables = _rope_tables(seq_len)

    x = jnp.concatenate([x_prompt.reshape(-1, D), x_sample.reshape(-1, D)], axis=0)
    for i in range(depth):
        x = _ffn(x, ffn_norm, wg, wu, wd, i, 0)
        j = i // 2
        if i % 2 == 0:
            qkv = _qkv(x, mix_norm[i], w_qkv[j], tables, n_seq, seq_len)
            outs, lses = [], []
            for g, (window, dil) in enumerate(DILATION_GROUPS):
                o, l = _attention(qkv[g], window, dil)
                outs.append(o)
                lses.append(l)
            x = _wo(x, outs, lses, w_o[j], n_seq, seq_len)
        else:
            u = _glu(x, mix_norm[i], w_in[j], conv_b_in[j])
            x = _conv_out(x, u, conv_w_dw[j], conv_b_dw[j], conv_ln_g[j], conv_ln_b[j],
                          w_out[j], conv_b_out[j], n_seq, seq_len)
        x = _ffn(x, ffn_norm, wg, wu, wd, i, 1,
                 final_g=final_norm if i == depth - 1 else None)
    n_p = n_prompt * seq_len
    return (x[:n_p].reshape(n_prompt, seq_len, D), x[n_p:].reshape(n_sample, seq_len, D))
```

```python
import functools

import jax
import jax.numpy as jnp
from jax import lax
from jax.experimental import pallas as pl
from jax.experimental.pallas import tpu as pltpu

F32 = jnp.float32
BF16 = jnp.bfloat16

HEAD_DIM = 128
ROT_DIM = HEAD_DIM // 4
ROPE_THETA = 500000.0
DILATION_GROUPS = ((128, 1), (512, 4), (2048, 16))
HEADS_PER_GROUP = 4
GROUP_WIDTH = HEADS_PER_GROUP * HEAD_DIM
N_GROUPS = len(DILATION_GROUPS)
ATTN_WIDTH = N_GROUPS * GROUP_WIDTH
CONV_WIDTH = 31
CONV_PAD = CONV_WIDTH // 2
FFN_RESIDUAL_WEIGHT = 0.5
RMS_EPS = 1e-6
LN_EPS = 1e-5
NEG_INF = -1e30

LANES = 128
SUBLANES = 8
VMEM_LIMIT_BYTES = 60 * 1024 * 1024

FFN_ROWS = 1024
FFN_COLS = 512
MIX_ROWS = 512
ATTN_ROWS = 1024
CONV_HALO = 16
CONV_CHUNK = 64


def _params(n_grid):
    return pltpu.CompilerParams(
        dimension_semantics=("arbitrary",) * n_grid,
        vmem_limit_bytes=VMEM_LIMIT_BYTES)


def _resident(shape, index_map):
    return pl.BlockSpec(shape, index_map, pipeline_mode=pl.Buffered(1))


def _rms(x, g):
    y = x * lax.rsqrt(jnp.mean(x * x, axis=-1, keepdims=True) + RMS_EPS)
    return y * g


def _sigmoid(x):
    return 1.0 / (1.0 + jnp.exp(-x))


def _ffn_kernel(x_ref, g_ref, wg_ref, wu_ref, wd_ref, fg_ref, o_ref, xn_ref, *, final_norm):
    f = pl.program_id(1)

    @pl.when(f == 0)
    def _():
        x = x_ref[...]
        xn_ref[...] = _rms(x, g_ref[...]).astype(BF16)
        o_ref[...] = x

    xn = xn_ref[...]
    gate = jnp.dot(xn, wg_ref[...], preferred_element_type=F32)
    up = jnp.dot(xn, wu_ref[...], preferred_element_type=F32)
    h = (gate * _sigmoid(gate) * up * FFN_RESIDUAL_WEIGHT).astype(BF16)
    o_ref[...] += jnp.dot(h, wd_ref[...], preferred_element_type=F32)

    if final_norm:
        @pl.when(f == pl.num_programs(1) - 1)
        def _():
            o_ref[...] = _rms(o_ref[...], fg_ref[...])


def _ffn(x, norm_g, w_gate, w_up, w_down, layer, which, tm, row0=0, rows=None, final_g=None):
    T, D = x.shape
    F = w_gate.shape[-1]
    rows = T if rows is None else rows
    tf = min(FFN_COLS, F)
    assert rows % tm == 0 and row0 % tm == 0 and F % tf == 0
    tile0 = row0 // tm
    final_norm = final_g is not None
    if final_g is None:
        final_g = norm_g[layer, which]
    kernel = functools.partial(_ffn_kernel, final_norm=final_norm)
    return pl.pallas_call(
        kernel,
        out_shape=jax.ShapeDtypeStruct((rows, D), F32),
        grid=(rows // tm, F // tf),
        in_specs=[
            pl.BlockSpec((tm, D), lambda i, f: (tile0 + i, 0)),
            pl.BlockSpec((None, None, 1, D), lambda i, f: (layer, which, 0, 0)),
            pl.BlockSpec((None, None, D, tf), lambda i, f: (layer, which, 0, f)),
            pl.BlockSpec((None, None, D, tf), lambda i, f: (layer, which, 0, f)),
            pl.BlockSpec((None, None, tf, D), lambda i, f: (layer, which, f, 0)),
            pl.BlockSpec((1, D), lambda i, f: (0, 0)),
        ],
        out_specs=pl.BlockSpec((tm, D), lambda i, f: (i, 0)),
        scratch_shapes=[pltpu.VMEM((tm, D), BF16)],
        compiler_params=_params(2),
        name="ffn",
    )(x, norm_g.reshape(norm_g.shape[0], 2, 1, D), w_gate, w_up, w_down,
      final_g.reshape(1, D))


def _rope_tables(seq_len):
    half = ROT_DIM // 2
    pos = jnp.arange(seq_len, dtype=F32)
    freqs = ROPE_THETA ** (-jnp.arange(0, ROT_DIM, 2, dtype=F32) / ROT_DIM)
    ang = pos[:, None] * freqs[None, :]
    cos, sin = jnp.cos(ang), jnp.sin(ang)
    zeros = jnp.zeros((seq_len, HEAD_DIM - ROT_DIM), F32)
    zhalf = jnp.zeros((seq_len, half), F32)
    c = jnp.concatenate([cos, cos, jnp.ones_like(zeros)], axis=-1)
    s_lo = jnp.concatenate([-sin, zhalf, zeros], axis=-1)
    s_hi = jnp.concatenate([zhalf, sin, zeros], axis=-1)
    return c, s_lo, s_hi


def _qkv_kernel(x_ref, g_ref, w_ref, c_ref, slo_ref, shi_ref, o0_ref, o1_ref, o2_ref,
                xn_ref, t_ref, *, tm):
    xn_ref[...] = _rms(x_ref[...], g_ref[...]).astype(BF16)
    xn = xn_ref[...]
    c, s_lo, s_hi = c_ref[...], slo_ref[...], shi_ref[...]
    half = ROT_DIM // 2
    scale = HEAD_DIM ** -0.5
    o_refs = (o0_ref, o1_ref, o2_ref)
    for part in range(3):
        for g, (_, dil) in enumerate(DILATION_GROUPS):
            col = part * ATTN_WIDTH + g * GROUP_WIDTH
            acc = jnp.dot(xn, w_ref[:, col:col + GROUP_WIDTH], preferred_element_type=F32)
            dst = o_refs[g]
            for j in range(HEADS_PER_GROUP):
                t = acc[:, j * HEAD_DIM:(j + 1) * HEAD_DIM]
                if part < 2:
                    t = (t * c + pltpu.roll(t, HEAD_DIM - half, 1) * s_lo
                         + pltpu.roll(t, half, 1) * s_hi)
                if part == 0:
                    t = t * scale
                cols = slice(part * GROUP_WIDTH + j * HEAD_DIM,
                             part * GROUP_WIDTH + (j + 1) * HEAD_DIM)
                if dil == 1:
                    dst[0, :, cols] = t.astype(BF16)
                else:
                    t_ref[j] = t
                    for r in range(dil):
                        dst[r, :, cols] = t_ref[j, pl.ds(r, tm // dil, stride=dil), :].astype(BF16)


def _qkv(x, norm_g, w_qkv, tables, n_seq, seq_len):
    T, D = x.shape
    tm = min(MIX_ROWS, seq_len)
    assert seq_len % tm == 0
    W = w_qkv.shape[-1]
    x3 = x.reshape(n_seq, seq_len, D)
    out_shapes, out_specs = [], []
    for _, dil in DILATION_GROUPS:
        assert tm % (dil * 16) == 0
        out_shapes.append(jax.ShapeDtypeStruct((n_seq, dil, seq_len // dil, 3 * GROUP_WIDTH), BF16))
        out_specs.append(pl.BlockSpec((None, dil, tm // dil, 3 * GROUP_WIDTH),
                                      lambda b, i: (b, 0, i, 0)))
    tab_spec = pl.BlockSpec((tm, HEAD_DIM), lambda b, i: (i, 0))
    return pl.pallas_call(
        functools.partial(_qkv_kernel, tm=tm),
        out_shape=out_shapes,
        grid=(n_seq, seq_len // tm),
        in_specs=[
            pl.BlockSpec((None, tm, D), lambda b, i: (b, i, 0)),
            pl.BlockSpec((1, D), lambda b, i: (0, 0)),
            _resident((D, W), lambda b, i: (0, 0)),
            tab_spec, tab_spec, tab_spec,
        ],
        out_specs=out_specs,
        scratch_shapes=[pltpu.VMEM((tm, D), BF16),
                        pltpu.VMEM((HEADS_PER_GROUP, tm, HEAD_DIM), F32)],
        compiler_params=_params(2),
        name="qkv_rope",
    )(x3, norm_g.reshape(1, D), w_qkv, *tables)


def _attn_kernel(q_ref, kp_ref, kc_ref, kn_ref, vp_ref, vc_ref, vn_ref, o_ref, lse_ref,
                 kbuf, vbuf, *, tq, half, sub_len):
    i = pl.program_id(2)
    kbuf[0:half] = kp_ref[...]
    kbuf[half:half + tq] = kc_ref[...]
    kbuf[half + tq:] = kn_ref[...]
    vbuf[0:half] = vp_ref[...]
    vbuf[half:half + tq] = vc_ref[...]
    vbuf[half + tq:] = vn_ref[...]

    sub = 2 * half
    win = sub + 2 * half
    row = lax.broadcasted_iota(jnp.int32, (sub, win), 0)
    col = lax.broadcasted_iota(jnp.int32, (sub, win), 1)
    band = (col - row).astype(jnp.uint32) <= jnp.uint32(2 * half)
    lane = lax.broadcasted_iota(jnp.int32, (sub, LANES), 1)
    for s in range(tq // sub):
        kpos = i * tq + (s * sub - half) + col
        inside = kpos.astype(jnp.uint32) < jnp.uint32(sub_len)
        bias = jnp.where(band, jnp.where(inside, 0.0, NEG_INF), NEG_INF).astype(F32)
        lse_tile = jnp.zeros((sub, LANES), F32)
        for j in range(HEADS_PER_GROUP):
            hs = slice(j * HEAD_DIM, (j + 1) * HEAD_DIM)
            q = q_ref[s * sub:(s + 1) * sub, hs]
            k = kbuf[s * sub:s * sub + win, hs]
            v = vbuf[s * sub:s * sub + win, hs]
            sc = lax.dot_general(q, k, (((1,), (1,)), ((), ())), preferred_element_type=F32)
            sc = sc + bias
            m = jnp.max(sc, axis=-1, keepdims=True)
            e = jnp.exp(sc - m)
            den = jnp.sum(e, axis=-1, keepdims=True)
            o = jnp.dot(e.astype(BF16), v, preferred_element_type=F32) / den
            o_ref[s * sub:(s + 1) * sub, hs] = o.astype(BF16)
            lse_tile = jnp.where(lane == j, m + jnp.log(den), lse_tile)
        lse_ref[s * sub:(s + 1) * sub, :] = lse_tile


def _attention(qkv_g, window, dil):
    n_seq, _, L, _ = qkv_g.shape
    half = window // (2 * dil)
    tq = min(ATTN_ROWS, L)
    assert L % tq == 0 and tq % (2 * half) == 0 and half % 16 == 0
    per = tq // half
    last = L // half - 1

    def cur(c):
        return pl.BlockSpec((None, None, tq, GROUP_WIDTH), lambda b, r, i: (b, r, i, c))

    def prev(c):
        return pl.BlockSpec((None, None, half, GROUP_WIDTH),
                            lambda b, r, i: (b, r, jnp.maximum(i * per - 1, 0), c))

    def nxt(c):
        return pl.BlockSpec((None, None, half, GROUP_WIDTH),
                            lambda b, r, i: (b, r, jnp.minimum((i + 1) * per, last), c))

    kernel = functools.partial(_attn_kernel, tq=tq, half=half, sub_len=L)
    return pl.pallas_call(
        kernel,
        out_shape=[jax.ShapeDtypeStruct((n_seq, dil, L, GROUP_WIDTH), BF16),
                   jax.ShapeDtypeStruct((n_seq, dil, L, LANES), F32)],
        grid=(n_seq, dil, L // tq),
        in_specs=[cur(0), prev(1), cur(1), nxt(1), prev(2), cur(2), nxt(2)],
        out_specs=[pl.BlockSpec((None, None, tq, GROUP_WIDTH), lambda b, r, i: (b, r, i, 0)),
                   pl.BlockSpec((None, None, tq, LANES), lambda b, r, i: (b, r, i, 0))],
        scratch_shapes=[pltpu.VMEM((tq + 2 * half, GROUP_WIDTH), BF16),
                        pltpu.VMEM((tq + 2 * half, GROUP_WIDTH), BF16)],
        compiler_params=_params(3),
        name=f"band_attn_d{dil}",
    )(qkv_g, qkv_g, qkv_g, qkv_g, qkv_g, qkv_g, qkv_g)


def _wo_kernel(x_ref, o0_ref, o1_ref, o2_ref, l0_ref, l1_ref, l2_ref, w_ref, out_ref,
               obuf, lbuf, *, tm):
    o_refs = (o0_ref, o1_ref, o2_ref)
    l_refs = (l0_ref, l1_ref, l2_ref)
    for g, (_, dil) in enumerate(DILATION_GROUPS):
        for r in range(dil):
            rows = pl.ds(r, tm // dil, stride=dil) if dil > 1 else slice(None)
            for j in range(HEADS_PER_GROUP):
                hs = slice(j * HEAD_DIM, (j + 1) * HEAD_DIM)
                obuf[g * HEADS_PER_GROUP + j, rows, :] = o_refs[g][r, :, hs].astype(F32)
            lbuf[g, rows, :] = l_refs[g][r]
    lse = [lbuf[g] for g in range(N_GROUPS)]
    m = jnp.maximum(jnp.maximum(lse[0], lse[1]), lse[2])
    e = [jnp.exp(l - m) for l in lse]
    inv = 1.0 / (e[0] + e[1] + e[2])
    acc = x_ref[...]
    for g in range(N_GROUPS):
        lam = e[g] * inv
        heads = [obuf[g * HEADS_PER_GROUP + j] * lam[:, j:j + 1]
                 for j in range(HEADS_PER_GROUP)]
        a = jnp.concatenate(heads, axis=-1).astype(BF16)
        acc = acc + jnp.dot(a, w_ref[g * GROUP_WIDTH:(g + 1) * GROUP_WIDTH, :],
                            preferred_element_type=F32)
    out_ref[...] = acc


def _wo(x, outs, lses, w_o, n_seq, seq_len):
    T, D = x.shape
    tm = min(MIX_ROWS, seq_len)
    x3 = x.reshape(n_seq, seq_len, D)
    o_specs, l_specs = [], []
    for _, dil in DILATION_GROUPS:
        o_specs.append(pl.BlockSpec((None, dil, tm // dil, GROUP_WIDTH), lambda b, i: (b, 0, i, 0)))
        l_specs.append(pl.BlockSpec((None, dil, tm // dil, LANES), lambda b, i: (b, 0, i, 0)))
    out = pl.pallas_call(
        functools.partial(_wo_kernel, tm=tm),
        out_shape=jax.ShapeDtypeStruct((n_seq, seq_len, D), F32),
        grid=(n_seq, seq_len // tm),
        in_specs=[pl.BlockSpec((None, tm, D), lambda b, i: (b, i, 0)), *o_specs, *l_specs,
                  _resident((ATTN_WIDTH, D), lambda b, i: (0, 0))],
        out_specs=pl.BlockSpec((None, tm, D), lambda b, i: (b, i, 0)),
        scratch_shapes=[pltpu.VMEM((N_GROUPS * HEADS_PER_GROUP, tm, HEAD_DIM), F32),
                        pltpu.VMEM((N_GROUPS, tm, LANES), F32)],
        compiler_params=_params(2),
        name="attn_out_proj",
    )(x3, *outs, *lses, w_o)
    return out.reshape(T, D)


def _glu_kernel(x_ref, g_ref, w_ref, b_ref, u_ref, xn_ref, *, n_ch, tn):
    xn_ref[...] = _rms(x_ref[...], g_ref[...]).astype(BF16)
    xn = xn_ref[...]
    for c in range(0, n_ch, tn):
        a = jnp.dot(xn, w_ref[:, c:c + tn], preferred_element_type=F32) + b_ref[:, c:c + tn]
        gt = (jnp.dot(xn, w_ref[:, n_ch + c:n_ch + c + tn], preferred_element_type=F32)
              + b_ref[:, n_ch + c:n_ch + c + tn])
        u_ref[:, c:c + tn] = a * _sigmoid(gt)


def _glu(x, norm_g, w_in, b_in):
    T, D = x.shape
    n_ch = w_in.shape[-1] // 2
    tm = min(MIX_ROWS, T)
    tn = min(512, n_ch)
    assert T % tm == 0 and n_ch % tn == 0
    return pl.pallas_call(
        functools.partial(_glu_kernel, n_ch=n_ch, tn=tn),
        out_shape=jax.ShapeDtypeStruct((T, n_ch), F32),
        grid=(T // tm,),
        in_specs=[pl.BlockSpec((tm, D), lambda i: (i, 0)),
                  pl.BlockSpec((1, D), lambda i: (0, 0)),
                  _resident((D, 2 * n_ch), lambda i: (0, 0)),
                  pl.BlockSpec((1, 2 * n_ch), lambda i: (0, 0))],
        out_specs=pl.BlockSpec((tm, n_ch), lambda i: (i, 0)),
        scratch_shapes=[pltpu.VMEM((tm, D), BF16)],
        compiler_params=_params(1),
        name="conv_in_glu",
    )(x, norm_g.reshape(1, D), w_in, b_in.reshape(1, 2 * n_ch))


def _conv_out_kernel(x_ref, up_ref, uc_ref, un_ref, wdw_ref, bdw_ref, lng_ref, lnb_ref,
                     w_ref, bo_ref, out_ref, ubuf, cbuf, hbuf, *, tm, n_ch, n_slices,
                     tiles_per_seq, n_tiles):
    s = pl.program_id(0)
    g = pl.program_id(1)
    i = lax.rem(jnp.minimum(s, n_tiles - 1), tiles_per_seq)
    first = i == 0
    last = i == tiles_per_seq - 1
    n_chunks = n_ch // LANES
    chunks_per_slice = n_chunks // n_slices

    @pl.when((s == 0) & (g == 0))
    def _():
        hbuf[...] = jnp.zeros_like(hbuf)

    @pl.when(g == 0)
    def _():
        for c in range(n_chunks):
            cs = slice(c * LANES, (c + 1) * LANES)
            ubuf[c, 0:CONV_HALO, :] = jnp.where(first, 0.0, up_ref[:, cs])
            ubuf[c, CONV_HALO:CONV_HALO + tm, :] = uc_ref[:, cs]
            ubuf[c, CONV_HALO + tm:, :] = jnp.where(last, 0.0, un_ref[:, cs])

    for cc in range(chunks_per_slice):
        c = g * chunks_per_slice + cc
        b = bdw_ref[c]
        for r0 in range(0, tm, CONV_CHUNK):
            acc = jnp.zeros((CONV_CHUNK, LANES), F32)
            for j in range(CONV_WIDTH):
                start = r0 + CONV_HALO - CONV_PAD + j
                w = jnp.tile(wdw_ref[c, j], (CONV_CHUNK // SUBLANES, 1))
                acc = acc + ubuf[c, start:start + CONV_CHUNK, :] * w
            cbuf[c, r0:r0 + CONV_CHUNK, :] = acc + b

    out_ref[...] = (x_ref[...] + bo_ref[...]
                    + jnp.dot(hbuf[...], w_ref[...], preferred_element_type=F32))

    @pl.when(g == n_slices - 1)
    def _():
        tot = cbuf[0]
        for c in range(1, n_chunks):
            tot = tot + cbuf[c]
        mu = jnp.sum(tot, axis=-1, keepdims=True) * (1.0 / n_ch)
        sq = jnp.zeros((tm, LANES), F32)
        for c in range(n_chunks):
            d = cbuf[c] - mu
            sq = sq + d * d
        var = jnp.sum(sq, axis=-1, keepdims=True) * (1.0 / n_ch)
        rstd = lax.rsqrt(var + LN_EPS)
        for c in range(n_chunks):
            cs = slice(c * LANES, (c + 1) * LANES)
            y = (cbuf[c] - mu) * rstd * lng_ref[:, cs] + lnb_ref[:, cs]
            hbuf[:, cs] = (y * _sigmoid(y)).astype(BF16)


def _conv_out(x, u, w_dw, b_dw, ln_g, ln_b, w_out, b_out, n_seq, seq_len):
    T, D = x.shape
    n_ch = u.shape[-1]
    tm = min(MIX_ROWS, seq_len)
    assert seq_len % tm == 0 and tm % CONV_HALO == 0 and tm % CONV_CHUNK == 0
    assert n_ch % LANES == 0 and D % LANES == 0 and CONV_HALO >= CONV_PAD
    n_chunks = n_ch // LANES
    n_slices = 4
    while n_chunks % n_slices or (D // LANES) % n_slices:
        n_slices //= 2
    per = tm // CONV_HALO
    last_halo = T // CONV_HALO - 1
    tiles_per_seq = seq_len // tm
    n_tiles = n_seq * tiles_per_seq

    def conv_tile(s):
        return jnp.minimum(s, n_tiles - 1)

    def proj_tile(s):
        return jnp.maximum(s - 1, 0)

    w_dw_c = w_dw.reshape(CONV_WIDTH, n_chunks, LANES).transpose(1, 0, 2)
    w_dw_c = jnp.broadcast_to(w_dw_c[:, :, None, :], (n_chunks, CONV_WIDTH, SUBLANES, LANES))
    b_dw_c = b_dw.reshape(n_chunks, 1, LANES)
    kernel = functools.partial(_conv_out_kernel, tm=tm, n_ch=n_ch, n_slices=n_slices,
                               tiles_per_seq=tiles_per_seq, n_tiles=n_tiles)
    return pl.pallas_call(
        kernel,
        out_shape=jax.ShapeDtypeStruct((T, D), F32),
        grid=(n_tiles + 1, n_slices),
        in_specs=[
            pl.BlockSpec((tm, D // n_slices), lambda s, g: (proj_tile(s), g)),
            pl.BlockSpec((CONV_HALO, n_ch),
                         lambda s, g: (jnp.maximum(conv_tile(s) * per - 1, 0), 0)),
            pl.BlockSpec((tm, n_ch), lambda s, g: (conv_tile(s), 0)),
            pl.BlockSpec((CONV_HALO, n_ch),
                         lambda s, g: (jnp.minimum((conv_tile(s) + 1) * per, last_halo), 0)),
            _resident((n_chunks, CONV_WIDTH, SUBLANES, LANES), lambda s, g: (0, 0, 0, 0)),
            pl.BlockSpec((n_chunks, 1, LANES), lambda s, g: (0, 0, 0)),
            pl.BlockSpec((1, n_ch), lambda s, g: (0, 0)),
            pl.BlockSpec((1, n_ch), lambda s, g: (0, 0)),
            pl.BlockSpec((n_ch, D // n_slices), lambda s, g: (0, g)),
            pl.BlockSpec((1, D // n_slices), lambda s, g: (0, g)),
        ],
        out_specs=pl.BlockSpec((tm, D // n_slices),
                               lambda s, g: (proj_tile(s), jnp.where(s == 0, 0, g))),
        scratch_shapes=[pltpu.VMEM((n_chunks, tm + 2 * CONV_HALO, LANES), F32),
                        pltpu.VMEM((n_chunks, tm, LANES), F32),
                        pltpu.VMEM((tm, n_ch), BF16)],
        compiler_params=_params(2),
        name="conv_ln_out",
    )(x, u, u, u, w_dw_c, b_dw_c, ln_g.reshape(1, n_ch), ln_b.reshape(1, n_ch),
      w_out, b_out.reshape(1, D))


def kernel(x_prompt, x_sample, ffn_norm, ffn_w_gate, ffn_w_up, ffn_w_down, mix_norm, attn_w_qkv, attn_w_o, conv_w_in, conv_b_in, conv_w_dw, conv_b_dw, conv_ln_g, conv_ln_b, conv_w_out, conv_b_out, final_norm):
    n_prompt, seq_len, D = x_prompt.shape
    n_sample = x_sample.shape[0]
    assert x_sample.shape[1:] == (seq_len, D)
    n_seq = n_prompt + n_sample
    depth = ffn_norm.shape[0]

    wg, wu, wd = (w.astype(BF16) for w in (ffn_w_gate, ffn_w_up, ffn_w_down))
    w_qkv, w_o = attn_w_qkv.astype(BF16), attn_w_o.astype(BF16)
    w_in, w_out = conv_w_in.astype(BF16), conv_w_out.astype(BF16)
    tables = _rope_tables(seq_len)

    n_p = n_prompt * seq_len
    n_s = n_sample * seq_len
    ffn_rows = FFN_ROWS
    while n_p % ffn_rows or n_s % ffn_rows:
        ffn_rows //= 2

    x = jnp.concatenate([x_prompt.reshape(-1, D), x_sample.reshape(-1, D)], axis=0)
    for i in range(depth):
        x = _ffn(x, ffn_norm, wg, wu, wd, i, 0, ffn_rows)
        j = i // 2
        if i % 2 == 0:
            qkv = _qkv(x, mix_norm[i], w_qkv[j], tables, n_seq, seq_len)
            outs, lses = [], []
            for g, (window, dil) in enumerate(DILATION_GROUPS):
                o, l = _attention(qkv[g], window, dil)
                outs.append(o)
                lses.append(l)
            x = _wo(x, outs, lses, w_o[j], n_seq, seq_len)
        else:
            u = _glu(x, mix_norm[i], w_in[j], conv_b_in[j])
            x = _conv_out(x, u, conv_w_dw[j], conv_b_dw[j], conv_ln_g[j], conv_ln_b[j],
                          w_out[j], conv_b_out[j], n_seq, seq_len)
        if i < depth - 1:
            x = _ffn(x, ffn_norm, wg, wu, wd, i, 1, ffn_rows)
    last = functools.partial(_ffn, x, ffn_norm, wg, wu, wd, depth - 1, 1, ffn_rows,
                             final_g=final_norm)
    y_prompt = last(row0=0, rows=n_p)
    y_sample = last(row0=n_p, rows=n_s)
    return (y_prompt.reshape(n_prompt, seq_len, D), y_sample.reshape(n_sample, seq_len, D))
```

```python
import functools

import jax
import jax.numpy as jnp
from jax import lax
from jax.experimental import pallas as pl
from jax.experimental.pallas import tpu as pltpu

F32 = jnp.float32
BF16 = jnp.bfloat16

HEAD_DIM = 128
ROT_DIM = HEAD_DIM // 4
ROPE_THETA = 500000.0
DILATION_GROUPS = ((128, 1), (512, 4), (2048, 16))
HEADS_PER_GROUP = 4
GROUP_WIDTH = HEADS_PER_GROUP * HEAD_DIM
N_GROUPS = len(DILATION_GROUPS)
ATTN_WIDTH = N_GROUPS * GROUP_WIDTH
CONV_WIDTH = 31
CONV_PAD = CONV_WIDTH // 2
FFN_RESIDUAL_WEIGHT = 0.5
RMS_EPS = 1e-6
LN_EPS = 1e-5
NEG_INF = -1e30

LANES = 128
VMEM_LIMIT_BYTES = 60 * 1024 * 1024

FFN_ROWS = 1024
FFN_COLS = 512
MIX_ROWS = 512
ATTN_ROWS = 1024
CONV_HALO = 16
CONV_CHUNK = 64


def _params(n_grid):
    return pltpu.CompilerParams(
        dimension_semantics=("arbitrary",) * n_grid,
        vmem_limit_bytes=VMEM_LIMIT_BYTES)


def _resident(shape, index_map):
    return pl.BlockSpec(shape, index_map, pipeline_mode=pl.Buffered(1))


def _rms(x, g):
    y = x * lax.rsqrt(jnp.mean(x * x, axis=-1, keepdims=True) + RMS_EPS)
    return y * g


def _sigmoid(x):
    return 1.0 / (1.0 + jnp.exp(-x))


def _ffn_kernel(x_ref, g_ref, wg_ref, wu_ref, wd_ref, fg_ref, o_ref, xn_ref, *, final_norm):
    f = pl.program_id(1)

    @pl.when(f == 0)
    def _():
        x = x_ref[...]
        xn_ref[...] = _rms(x, g_ref[...]).astype(BF16)
        o_ref[...] = x

    xn = xn_ref[...]
    gate = jnp.dot(xn, wg_ref[...], preferred_element_type=F32)
    up = jnp.dot(xn, wu_ref[...], preferred_element_type=F32)
    h = (gate * _sigmoid(gate) * up * FFN_RESIDUAL_WEIGHT).astype(BF16)
    o_ref[...] += jnp.dot(h, wd_ref[...], preferred_element_type=F32)

    if final_norm:
        @pl.when(f == pl.num_programs(1) - 1)
        def _():
            o_ref[...] = _rms(o_ref[...], fg_ref[...])


def _ffn(x, norm_g, w_gate, w_up, w_down, layer, which, tm, row0=0, rows=None, final_g=None):
    T, D = x.shape
    F = w_gate.shape[-1]
    rows = T if rows is None else rows
    tf = min(FFN_COLS, F)
    assert rows % tm == 0 and row0 % tm == 0 and F % tf == 0
    tile0 = row0 // tm
    final_norm = final_g is not None
    if final_g is None:
        final_g = norm_g[layer, which]
    kernel = functools.partial(_ffn_kernel, final_norm=final_norm)
    return pl.pallas_call(
        kernel,
        out_shape=jax.ShapeDtypeStruct((rows, D), F32),
        grid=(rows // tm, F // tf),
        in_specs=[
            pl.BlockSpec((tm, D), lambda i, f: (tile0 + i, 0)),
            pl.BlockSpec((None, None, 1, D), lambda i, f: (layer, which, 0, 0)),
            pl.BlockSpec((None, None, D, tf), lambda i, f: (layer, which, 0, f)),
            pl.BlockSpec((None, None, D, tf), lambda i, f: (layer, which, 0, f)),
            pl.BlockSpec((None, None, tf, D), lambda i, f: (layer, which, f, 0)),
            pl.BlockSpec((1, D), lambda i, f: (0, 0)),
        ],
        out_specs=pl.BlockSpec((tm, D), lambda i, f: (i, 0)),
        scratch_shapes=[pltpu.VMEM((tm, D), BF16)],
        compiler_params=_params(2),
        name="ffn",
    )(x, norm_g.reshape(norm_g.shape[0], 2, 1, D), w_gate, w_up, w_down,
      final_g.reshape(1, D))


def _rope_tables(seq_len):
    half = ROT_DIM // 2
    pos = jnp.arange(seq_len, dtype=F32)
    freqs = ROPE_THETA ** (-jnp.arange(0, ROT_DIM, 2, dtype=F32) / ROT_DIM)
    ang = pos[:, None] * freqs[None, :]
    cos, sin = jnp.cos(ang), jnp.sin(ang)
    zeros = jnp.zeros((seq_len, HEAD_DIM - ROT_DIM), F32)
    zhalf = jnp.zeros((seq_len, half), F32)
    c = jnp.concatenate([cos, cos, jnp.ones_like(zeros)], axis=-1)
    s_lo = jnp.concatenate([-sin, zhalf, zeros], axis=-1)
    s_hi = jnp.concatenate([zhalf, sin, zeros], axis=-1)
    return c, s_lo, s_hi


def _qkv_kernel(x_ref, g_ref, w_ref, c_ref, slo_ref, shi_ref, o0_ref, o1_ref, o2_ref,
                xn_ref, t_ref, *, tm):
    xn_ref[...] = _rms(x_ref[...], g_ref[...]).astype(BF16)
    xn = xn_ref[...]
    c, s_lo, s_hi = c_ref[...], slo_ref[...], shi_ref[...]
    half = ROT_DIM // 2
    scale = HEAD_DIM ** -0.5
    o_refs = (o0_ref, o1_ref, o2_ref)
    for part in range(3):
        for g, (_, dil) in reversed(list(enumerate(DILATION_GROUPS))):
            col = part * ATTN_WIDTH + g * GROUP_WIDTH
            acc = jnp.dot(xn, w_ref[:, col:col + GROUP_WIDTH], preferred_element_type=F32)
            dst = o_refs[g]
            for j in range(HEADS_PER_GROUP):
                t = acc[:, j * HEAD_DIM:(j + 1) * HEAD_DIM]
                if part < 2:
                    t = (t * c + pltpu.roll(t, HEAD_DIM - half, 1) * s_lo
                         + pltpu.roll(t, half, 1) * s_hi)
                if part == 0:
                    t = t * scale
                cols = slice(part * GROUP_WIDTH + j * HEAD_DIM,
                             part * GROUP_WIDTH + (j + 1) * HEAD_DIM)
                if dil == 1:
                    dst[0, :, cols] = t.astype(BF16)
                else:
                    t_ref[j] = t
                    for r in range(dil):
                        dst[r, :, cols] = t_ref[j, pl.ds(r, tm // dil, stride=dil), :].astype(BF16)


def _qkv(x, norm_g, w_qkv, tables, n_seq, seq_len):
    T, D = x.shape
    tm = min(MIX_ROWS, seq_len)
    assert seq_len % tm == 0
    W = w_qkv.shape[-1]
    x3 = x.reshape(n_seq, seq_len, D)
    out_shapes, out_specs = [], []
    for _, dil in DILATION_GROUPS:
        assert tm % (dil * 16) == 0
        out_shapes.append(jax.ShapeDtypeStruct((n_seq, dil, seq_len // dil, 3 * GROUP_WIDTH), BF16))
        out_specs.append(pl.BlockSpec((None, dil, tm // dil, 3 * GROUP_WIDTH),
                                      lambda b, i: (b, 0, i, 0)))
    tab_spec = pl.BlockSpec((tm, HEAD_DIM), lambda b, i: (i, 0))
    return pl.pallas_call(
        functools.partial(_qkv_kernel, tm=tm),
        out_shape=out_shapes,
        grid=(n_seq, seq_len // tm),
        in_specs=[
            pl.BlockSpec((None, tm, D), lambda b, i: (b, i, 0)),
            pl.BlockSpec((1, D), lambda b, i: (0, 0)),
            _resident((D, W), lambda b, i: (0, 0)),
            tab_spec, tab_spec, tab_spec,
        ],
        out_specs=out_specs,
        scratch_shapes=[pltpu.VMEM((tm, D), BF16),
                        pltpu.VMEM((HEADS_PER_GROUP, tm, HEAD_DIM), F32)],
        compiler_params=_params(2),
        name="qkv_rope",
    )(x3, norm_g.reshape(1, D), w_qkv, *tables)


def _attn_kernel(q_ref, kp_ref, kc_ref, kn_ref, vp_ref, vc_ref, vn_ref, o_ref, lse_ref,
                 kbuf, vbuf, *, tq, half, sub_len):
    i = pl.program_id(2)
    kbuf[0:half] = kp_ref[...]
    kbuf[half:half + tq] = kc_ref[...]
    kbuf[half + tq:] = kn_ref[...]
    vbuf[0:half] = vp_ref[...]
    vbuf[half:half + tq] = vc_ref[...]
    vbuf[half + tq:] = vn_ref[...]

    sub = 2 * half
    win = sub + 2 * half
    row = lax.broadcasted_iota(jnp.int32, (sub, win), 0)
    col = lax.broadcasted_iota(jnp.int32, (sub, win), 1)
    band = (col - row).astype(jnp.uint32) <= jnp.uint32(2 * half)
    lane = lax.broadcasted_iota(jnp.int32, (sub, LANES), 1)
    for s in range(tq // sub):
        kpos = i * tq + (s * sub - half) + col
        inside = kpos.astype(jnp.uint32) < jnp.uint32(sub_len)
        bias = jnp.where(band, jnp.where(inside, 0.0, NEG_INF), NEG_INF).astype(F32)
        lse_tile = jnp.zeros((sub, LANES), F32)
        for j in range(HEADS_PER_GROUP):
            hs = slice(j * HEAD_DIM, (j + 1) * HEAD_DIM)
            q = q_ref[s * sub:(s + 1) * sub, hs]
            k = kbuf[s * sub:s * sub + win, hs]
            v = vbuf[s * sub:s * sub + win, hs]
            sc = lax.dot_general(q, k, (((1,), (1,)), ((), ())), preferred_element_type=F32)
            sc = sc + bias
            m = jnp.max(sc, axis=-1, keepdims=True)
            e = jnp.exp(sc - m)
            den = jnp.sum(e, axis=-1, keepdims=True)
            o = jnp.dot(e.astype(BF16), v, preferred_element_type=F32) / den
            o_ref[s * sub:(s + 1) * sub, hs] = o.astype(BF16)
            lse_tile = jnp.where(lane == j, m + jnp.log(den), lse_tile)
        lse_ref[s * sub:(s + 1) * sub, :] = lse_tile


def _attention(qkv_g, window, dil):
    n_seq, _, L, _ = qkv_g.shape
    half = window // (2 * dil)
    tq = min(ATTN_ROWS, L)
    assert L % tq == 0 and tq % (2 * half) == 0 and half % 16 == 0
    per = tq // half
    last = L // half - 1

    def cur(c):
        return pl.BlockSpec((None, None, tq, GROUP_WIDTH), lambda b, r, i: (b, r, i, c))

    def prev(c):
        return pl.BlockSpec((None, None, half, GROUP_WIDTH),
                            lambda b, r, i: (b, r, jnp.maximum(i * per - 1, 0), c))

    def nxt(c):
        return pl.BlockSpec((None, None, half, GROUP_WIDTH),
                            lambda b, r, i: (b, r, jnp.minimum((i + 1) * per, last), c))

    kernel = functools.partial(_attn_kernel, tq=tq, half=half, sub_len=L)
    return pl.pallas_call(
        kernel,
        out_shape=[jax.ShapeDtypeStruct((n_seq, dil, L, GROUP_WIDTH), BF16),
                   jax.ShapeDtypeStruct((n_seq, dil, L, LANES), F32)],
        grid=(n_seq, dil, L // tq),
        in_specs=[cur(0), prev(1), cur(1), nxt(1), prev(2), cur(2), nxt(2)],
        out_specs=[pl.BlockSpec((None, None, tq, GROUP_WIDTH), lambda b, r, i: (b, r, i, 0)),
                   pl.BlockSpec((None, None, tq, LANES), lambda b, r, i: (b, r, i, 0))],
        scratch_shapes=[pltpu.VMEM((tq + 2 * half, GROUP_WIDTH), BF16),
                        pltpu.VMEM((tq + 2 * half, GROUP_WIDTH), BF16)],
        compiler_params=_params(3),
        name=f"band_attn_d{dil}",
    )(qkv_g, qkv_g, qkv_g, qkv_g, qkv_g, qkv_g, qkv_g)


def _wo_kernel(x_ref, o0_ref, o1_ref, o2_ref, l0_ref, l1_ref, l2_ref, w_ref, out_ref,
               obuf, lbuf, *, tm):
    o_refs = (o0_ref, o1_ref, o2_ref)
    l_refs = (l0_ref, l1_ref, l2_ref)
    for g, (_, dil) in enumerate(DILATION_GROUPS):
        for r in range(dil):
            rows = pl.ds(r, tm // dil, stride=dil) if dil > 1 else slice(None)
            for j in range(HEADS_PER_GROUP):
                hs = slice(j * HEAD_DIM, (j + 1) * HEAD_DIM)
                obuf[g * HEADS_PER_GROUP + j, rows, :] = o_refs[g][r, :, hs].astype(F32)
            lbuf[g, rows, :] = l_refs[g][r]
    lse = [lbuf[g] for g in range(N_GROUPS)]
    m = jnp.maximum(jnp.maximum(lse[0], lse[1]), lse[2])
    e = [jnp.exp(l - m) for l in lse]
    inv = 1.0 / (e[0] + e[1] + e[2])
    acc = x_ref[...]
    for g in range(N_GROUPS):
        lam = e[g] * inv
        heads = [obuf[g * HEADS_PER_GROUP + j] * lam[:, j:j + 1]
                 for j in range(HEADS_PER_GROUP)]
        a = jnp.concatenate(heads, axis=-1).astype(BF16)
        acc = acc + jnp.dot(a, w_ref[g * GROUP_WIDTH:(g + 1) * GROUP_WIDTH, :],
                            preferred_element_type=F32)
    out_ref[...] = acc


def _wo(x, outs, lses, w_o, n_seq, seq_len):
    T, D = x.shape
    tm = min(MIX_ROWS, seq_len)
    x3 = x.reshape(n_seq, seq_len, D)
    o_specs, l_specs = [], []
    for _, dil in DILATION_GROUPS:
        o_specs.append(pl.BlockSpec((None, dil, tm // dil, GROUP_WIDTH), lambda b, i: (b, 0, i, 0)))
        l_specs.append(pl.BlockSpec((None, dil, tm // dil, LANES), lambda b, i: (b, 0, i, 0)))
    out = pl.pallas_call(
        functools.partial(_wo_kernel, tm=tm),
        out_shape=jax.ShapeDtypeStruct((n_seq, seq_len, D), F32),
        grid=(n_seq, seq_len // tm),
        in_specs=[pl.BlockSpec((None, tm, D), lambda b, i: (b, i, 0)), *o_specs, *l_specs,
                  _resident((ATTN_WIDTH, D), lambda b, i: (0, 0))],
        out_specs=pl.BlockSpec((None, tm, D), lambda b, i: (b, i, 0)),
        scratch_shapes=[pltpu.VMEM((N_GROUPS * HEADS_PER_GROUP, tm, HEAD_DIM), F32),
                        pltpu.VMEM((N_GROUPS, tm, LANES), F32)],
        compiler_params=_params(2),
        name="attn_out_proj",
    )(x3, *outs, *lses, w_o)
    return out.reshape(T, D)


def _glu_kernel(x_ref, g_ref, w_ref, b_ref, u_ref, xn_ref, *, n_ch, tn):
    xn_ref[...] = _rms(x_ref[...], g_ref[...]).astype(BF16)
    xn = xn_ref[...]
    for c in range(0, n_ch, tn):
        a = jnp.dot(xn, w_ref[:, c:c + tn], preferred_element_type=F32) + b_ref[:, c:c + tn]
        gt = (jnp.dot(xn, w_ref[:, n_ch + c:n_ch + c + tn], preferred_element_type=F32)
              + b_ref[:, n_ch + c:n_ch + c + tn])
        u_ref[:, c:c + tn] = a * _sigmoid(gt)


def _glu(x, norm_g, w_in, b_in):
    T, D = x.shape
    n_ch = w_in.shape[-1] // 2
    tm = min(MIX_ROWS, T)
    tn = min(512, n_ch)
    assert T % tm == 0 and n_ch % tn == 0
    return pl.pallas_call(
        functools.partial(_glu_kernel, n_ch=n_ch, tn=tn),
        out_shape=jax.ShapeDtypeStruct((T, n_ch), F32),
        grid=(T // tm,),
        in_specs=[pl.BlockSpec((tm, D), lambda i: (i, 0)),
                  pl.BlockSpec((1, D), lambda i: (0, 0)),
                  _resident((D, 2 * n_ch), lambda i: (0, 0)),
                  pl.BlockSpec((1, 2 * n_ch), lambda i: (0, 0))],
        out_specs=pl.BlockSpec((tm, n_ch), lambda i: (i, 0)),
        scratch_shapes=[pltpu.VMEM((tm, D), BF16)],
        compiler_params=_params(1),
        name="conv_in_glu",
    )(x, norm_g.reshape(1, D), w_in, b_in.reshape(1, 2 * n_ch))


def _conv_out_kernel(x_ref, up_ref, uc_ref, un_ref, wdw_ref, bdw_ref, lng_ref, lnb_ref,
                     w_ref, bo_ref, out_ref, ubuf, cbuf, hbuf, *, tm, n_ch):
    i = pl.program_id(1)
    n_chunks = n_ch // LANES
    first = i == 0
    last = i == pl.num_programs(1) - 1
    for c in range(n_chunks):
        cs = slice(c * LANES, (c + 1) * LANES)
        ubuf[c, 0:CONV_HALO, :] = jnp.where(first, 0.0, up_ref[:, cs])
        ubuf[c, CONV_HALO:CONV_HALO + tm, :] = uc_ref[:, cs]
        ubuf[c, CONV_HALO + tm:, :] = jnp.where(last, 0.0, un_ref[:, cs])

    def conv_chunk(c, carry):
        w = wdw_ref[c]
        b = bdw_ref[c]
        for r0 in range(0, tm, CONV_CHUNK):
            acc = jnp.zeros((CONV_CHUNK, LANES), F32)
            for j in range(CONV_WIDTH):
                start = r0 + CONV_HALO - CONV_PAD + j
                acc = acc + ubuf[c, start:start + CONV_CHUNK, :] * w[j:j + 1, :]
            cbuf[c, r0:r0 + CONV_CHUNK, :] = acc + b
        return carry

    lax.fori_loop(0, n_chunks, conv_chunk, 0)

    tot = cbuf[0]
    for c in range(1, n_chunks):
        tot = tot + cbuf[c]
    mu = jnp.sum(tot, axis=-1, keepdims=True) * (1.0 / n_ch)
    sq = jnp.zeros((tm, LANES), F32)
    for c in range(n_chunks):
        d = cbuf[c] - mu
        sq = sq + d * d
    var = jnp.sum(sq, axis=-1, keepdims=True) * (1.0 / n_ch)
    rstd = lax.rsqrt(var + LN_EPS)
    for c in range(n_chunks):
        cs = slice(c * LANES, (c + 1) * LANES)
        y = (cbuf[c] - mu) * rstd * lng_ref[:, cs] + lnb_ref[:, cs]
        hbuf[:, cs] = (y * _sigmoid(y)).astype(BF16)

    out_ref[...] = (x_ref[...] + bo_ref[...]
                    + jnp.dot(hbuf[...], w_ref[...], preferred_element_type=F32))


def _conv_out(x, u, w_dw, b_dw, ln_g, ln_b, w_out, b_out, n_seq, seq_len):
    T, D = x.shape
    n_ch = u.shape[-1]
    tm = min(MIX_ROWS, seq_len)
    assert seq_len % tm == 0 and tm % CONV_HALO == 0 and tm % CONV_CHUNK == 0
    assert n_ch % LANES == 0 and CONV_HALO >= CONV_PAD
    n_chunks = n_ch // LANES
    per = tm // CONV_HALO
    last = seq_len // CONV_HALO - 1
    x3 = x.reshape(n_seq, seq_len, D)
    u3 = u.reshape(n_seq, seq_len, n_ch)
    w_dw_c = w_dw.reshape(CONV_WIDTH, n_chunks, LANES).transpose(1, 0, 2)
    b_dw_c = b_dw.reshape(n_chunks, 1, LANES)
    out = pl.pallas_call(
        functools.partial(_conv_out_kernel, tm=tm, n_ch=n_ch),
        out_shape=jax.ShapeDtypeStruct((n_seq, seq_len, D), F32),
        grid=(n_seq, seq_len // tm),
        in_specs=[
            pl.BlockSpec((None, tm, D), lambda b, i: (b, i, 0)),
            pl.BlockSpec((None, CONV_HALO, n_ch),
                         lambda b, i: (b, jnp.maximum(i * per - 1, 0), 0)),
            pl.BlockSpec((None, tm, n_ch), lambda b, i: (b, i, 0)),
            pl.BlockSpec((None, CONV_HALO, n_ch),
                         lambda b, i: (b, jnp.minimum((i + 1) * per, last), 0)),
            pl.BlockSpec((n_chunks, CONV_WIDTH, LANES), lambda b, i: (0, 0, 0)),
            pl.BlockSpec((n_chunks, 1, LANES), lambda b, i: (0, 0, 0)),
            pl.BlockSpec((1, n_ch), lambda b, i: (0, 0)),
            pl.BlockSpec((1, n_ch), lambda b, i: (0, 0)),
            _resident((n_ch, D), lambda b, i: (0, 0)),
            pl.BlockSpec((1, D), lambda b, i: (0, 0)),
        ],
        out_specs=pl.BlockSpec((None, tm, D), lambda b, i: (b, i, 0)),
        scratch_shapes=[pltpu.VMEM((n_chunks, tm + 2 * CONV_HALO, LANES), F32),
                        pltpu.VMEM((n_chunks, tm, LANES), F32),
                        pltpu.VMEM((tm, n_ch), BF16)],
        compiler_params=_params(2),
        name="conv_ln_out",
    )(x3, u3, u3, u3, w_dw_c, b_dw_c, ln_g.reshape(1, n_ch), ln_b.reshape(1, n_ch),
      w_out, b_out.reshape(1, D))
    return out.reshape(T, D)


def kernel(x_prompt, x_sample, ffn_norm, ffn_w_gate, ffn_w_up, ffn_w_down, mix_norm, attn_w_qkv, attn_w_o, conv_w_in, conv_b_in, conv_w_dw, conv_b_dw, conv_ln_g, conv_ln_b, conv_w_out, conv_b_out, final_norm):
    n_prompt, seq_len, D = x_prompt.shape
    n_sample = x_sample.shape[0]
    assert x_sample.shape[1:] == (seq_len, D)
    n_seq = n_prompt + n_sample
    depth = ffn_norm.shape[0]

    wg, wu, wd = (w.astype(BF16) for w in (ffn_w_gate, ffn_w_up, ffn_w_down))
    w_qkv, w_o = attn_w_qkv.astype(BF16), attn_w_o.astype(BF16)
    w_in, w_out = conv_w_in.astype(BF16), conv_w_out.astype(BF16)
    tables = _rope_tables(seq_len)

    n_p = n_prompt * seq_len
    n_s = n_sample * seq_len
    ffn_rows = FFN_ROWS
    while n_p % ffn_rows or n_s % ffn_rows:
        ffn_rows //= 2

    x = jnp.concatenate([x_prompt.reshape(-1, D), x_sample.reshape(-1, D)], axis=0)
    for i in range(depth):
        x = _ffn(x, ffn_norm, wg, wu, wd, i, 0, ffn_rows)
        j = i // 2
        if i % 2 == 0:
            qkv = _qkv(x, mix_norm[i], w_qkv[j], tables, n_seq, seq_len)
            outs, lses = [], []
            for g, (window, dil) in enumerate(DILATION_GROUPS):
                o, l = _attention(qkv[g], window, dil)
                outs.append(o)
                lses.append(l)
            x = _wo(x, outs, lses, w_o[j], n_seq, seq_len)
        else:
            u = _glu(x, mix_norm[i], w_in[j], conv_b_in[j])
            x = _conv_out(x, u, conv_w_dw[j], conv_b_dw[j], conv_ln_g[j], conv_ln_b[j],
                          w_out[j], conv_b_out[j], n_seq, seq_len)
        if i < depth - 1:
            x = _ffn(x, ffn_norm, wg, wu, wd, i, 1, ffn_rows)
    last = functools.partial(_ffn, x, ffn_norm, wg, wu, wd, depth - 1, 1, ffn_rows,
                             final_g=final_norm)
    y_prompt = last(row0=0, rows=n_p)
    y_sample = last(row0=n_p, rows=n_s)
    return (y_prompt.reshape(n_prompt, seq_len, D), y_sample.reshape(n_sample, seq_len, D))
```

```python
import functools

import jax
import jax.numpy as jnp
from jax import lax
from jax.experimental import pallas as pl
from jax.experimental.pallas import tpu as pltpu

F32 = jnp.float32
BF16 = jnp.bfloat16

HEAD_DIM = 128
ROT_DIM = HEAD_DIM // 4
ROPE_THETA = 500000.0
DILATION_GROUPS = ((128, 1), (512, 4), (2048, 16))
HEADS_PER_GROUP = 4
GROUP_WIDTH = HEADS_PER_GROUP * HEAD_DIM
N_GROUPS = len(DILATION_GROUPS)
ATTN_WIDTH = N_GROUPS * GROUP_WIDTH
CONV_WIDTH = 31
CONV_PAD = CONV_WIDTH // 2
FFN_RESIDUAL_WEIGHT = 0.5
RMS_EPS = 1e-6
LN_EPS = 1e-5
NEG_INF = -1e30

LANES = 128
VMEM_LIMIT_BYTES = 60 * 1024 * 1024

FFN_ROWS = 1024
FFN_COLS = 512
MIX_ROWS = 512
ATTN_ROWS = 1024
CONV_HALO = 16
CONV_CHUNK = 64


def _params(n_grid):
    return pltpu.CompilerParams(
        dimension_semantics=("arbitrary",) * n_grid,
        vmem_limit_bytes=VMEM_LIMIT_BYTES)


def _resident(shape, index_map):
    return pl.BlockSpec(shape, index_map, pipeline_mode=pl.Buffered(1))


def _rms(x, g):
    y = x * lax.rsqrt(jnp.mean(x * x, axis=-1, keepdims=True) + RMS_EPS)
    return y * g


def _sigmoid(x):
    return 1.0 / (1.0 + jnp.exp(-x))


def _ffn_kernel(x_ref, g_ref, wg_ref, wu_ref, wd_ref, fg_ref, o_ref, xn_ref, *, final_norm):
    f = pl.program_id(1)

    @pl.when(f == 0)
    def _():
        x = x_ref[...]
        xn_ref[...] = _rms(x, g_ref[...]).astype(BF16)
        o_ref[...] = x

    xn = xn_ref[...]
    gate = jnp.dot(xn, wg_ref[...], preferred_element_type=F32)
    up = jnp.dot(xn, wu_ref[...], preferred_element_type=F32)
    h = (gate * _sigmoid(gate) * up * FFN_RESIDUAL_WEIGHT).astype(BF16)
    o_ref[...] += jnp.dot(h, wd_ref[...], preferred_element_type=F32)

    if final_norm:
        @pl.when(f == pl.num_programs(1) - 1)
        def _():
            o_ref[...] = _rms(o_ref[...], fg_ref[...])


def _ffn(x, norm_g, w_gate, w_up, w_down, layer, which, tm, row0=0, rows=None, final_g=None):
    T, D = x.shape
    F = w_gate.shape[-1]
    rows = T if rows is None else rows
    tf = min(FFN_COLS, F)
    assert rows % tm == 0 and row0 % tm == 0 and F % tf == 0
    tile0 = row0 // tm
    final_norm = final_g is not None
    if final_g is None:
        final_g = norm_g[layer, which]
    kernel = functools.partial(_ffn_kernel, final_norm=final_norm)
    return pl.pallas_call(
        kernel,
        out_shape=jax.ShapeDtypeStruct((rows, D), F32),
        grid=(rows // tm, F // tf),
        in_specs=[
            pl.BlockSpec((tm, D), lambda i, f: (tile0 + i, 0)),
            pl.BlockSpec((None, None, 1, D), lambda i, f: (layer, which, 0, 0)),
            pl.BlockSpec((None, None, D, tf), lambda i, f: (layer, which, 0, f)),
            pl.BlockSpec((None, None, D, tf), lambda i, f: (layer, which, 0, f)),
            pl.BlockSpec((None, None, tf, D), lambda i, f: (layer, which, f, 0)),
            pl.BlockSpec((1, D), lambda i, f: (0, 0)),
        ],
        out_specs=pl.BlockSpec((tm, D), lambda i, f: (i, 0)),
        scratch_shapes=[pltpu.VMEM((tm, D), BF16)],
        compiler_params=_params(2),
        name="ffn",
    )(x, norm_g.reshape(norm_g.shape[0], 2, 1, D), w_gate, w_up, w_down,
      final_g.reshape(1, D))


def _rope_tables(seq_len):
    half = ROT_DIM // 2
    pos = jnp.arange(seq_len, dtype=F32)
    freqs = ROPE_THETA ** (-jnp.arange(0, ROT_DIM, 2, dtype=F32) / ROT_DIM)
    ang = pos[:, None] * freqs[None, :]
    cos, sin = jnp.cos(ang), jnp.sin(ang)
    zeros = jnp.zeros((seq_len, HEAD_DIM - ROT_DIM), F32)
    zhalf = jnp.zeros((seq_len, half), F32)
    c = jnp.concatenate([cos, cos, jnp.ones_like(zeros)], axis=-1)
    s_lo = jnp.concatenate([-sin, zhalf, zeros], axis=-1)
    s_hi = jnp.concatenate([zhalf, sin, zeros], axis=-1)
    return c, s_lo, s_hi


def _part_specs(parts, seq_len, tm, D):
    specs, arrays, bounds = [], [], []
    start = 0
    for arr in parts:
        n = arr.shape[0] // seq_len
        last_tile = seq_len // tm - 1

        def index_map(b, i, start=start, n=n):
            tile = jnp.where(b < start, 0, jnp.where(b >= start + n, last_tile, i))
            return (jnp.clip(b - start, 0, n - 1), tile, 0)

        specs.append(pl.BlockSpec((None, tm, D), index_map))
        arrays.append(arr.reshape(n, seq_len, D))
        bounds.append((start, start + n))
        start += n
    return specs, arrays, bounds


def _current_part(x_refs, bounds):
    b = pl.program_id(0)
    x = x_refs[-1][...]
    for x_ref, (_, hi) in reversed(list(zip(x_refs[:-1], bounds[:-1]))):
        x = jnp.where(b < hi, x_ref[...], x)
    return x


def _qkv_kernel(*refs, tm, bounds):
    n_x = len(bounds)
    x_refs = refs[:n_x]
    (g_ref, w_ref, c_ref, slo_ref, shi_ref, o0_ref, o1_ref, o2_ref, xn_ref, t_ref) = refs[n_x:]

    xn_ref[...] = _rms(_current_part(x_refs, bounds), g_ref[...]).astype(BF16)
    xn = xn_ref[...]
    c, s_lo, s_hi = c_ref[...], slo_ref[...], shi_ref[...]
    half = ROT_DIM // 2
    scale = HEAD_DIM ** -0.5
    o_refs = (o0_ref, o1_ref, o2_ref)
    for part in range(3):
        for g, (_, dil) in reversed(list(enumerate(DILATION_GROUPS))):
            col = part * ATTN_WIDTH + g * GROUP_WIDTH
            acc = jnp.dot(xn, w_ref[:, col:col + GROUP_WIDTH], preferred_element_type=F32)
            dst = o_refs[g]
            for j in range(HEADS_PER_GROUP):
                t = acc[:, j * HEAD_DIM:(j + 1) * HEAD_DIM]
                if part < 2:
                    t = (t * c + pltpu.roll(t, HEAD_DIM - half, 1) * s_lo
                         + pltpu.roll(t, half, 1) * s_hi)
                if part == 0:
                    t = t * scale
                cols = slice(part * GROUP_WIDTH + j * HEAD_DIM,
                             part * GROUP_WIDTH + (j + 1) * HEAD_DIM)
                if dil == 1:
                    dst[0, :, cols] = t.astype(BF16)
                else:
                    t_ref[j] = t
                    for r in range(dil):
                        dst[r, :, cols] = t_ref[j, pl.ds(r, tm // dil, stride=dil), :].astype(BF16)


def _qkv(parts, norm_g, w_qkv, tables, n_seq, seq_len):
    D = parts[0].shape[-1]
    tm = min(MIX_ROWS, seq_len)
    assert seq_len % tm == 0
    W = w_qkv.shape[-1]
    x_specs, x_arrays, bounds = _part_specs(parts, seq_len, tm, D)
    out_shapes, out_specs = [], []
    for _, dil in DILATION_GROUPS:
        assert tm % (dil * 16) == 0
        out_shapes.append(jax.ShapeDtypeStruct((n_seq, dil, seq_len // dil, 3 * GROUP_WIDTH), BF16))
        out_specs.append(pl.BlockSpec((None, dil, tm // dil, 3 * GROUP_WIDTH),
                                      lambda b, i: (b, 0, i, 0)))
    tab_spec = pl.BlockSpec((tm, HEAD_DIM), lambda b, i: (i, 0))
    return pl.pallas_call(
        functools.partial(_qkv_kernel, tm=tm, bounds=bounds),
        out_shape=out_shapes,
        grid=(n_seq, seq_len // tm),
        in_specs=[
            *x_specs,
            pl.BlockSpec((1, D), lambda b, i: (0, 0)),
            _resident((D, W), lambda b, i: (0, 0)),
            tab_spec, tab_spec, tab_spec,
        ],
        out_specs=out_specs,
        scratch_shapes=[pltpu.VMEM((tm, D), BF16),
                        pltpu.VMEM((HEADS_PER_GROUP, tm, HEAD_DIM), F32)],
        compiler_params=_params(2),
        name="qkv_rope",
    )(*x_arrays, norm_g.reshape(1, D), w_qkv, *tables)


def _attn_kernel(q_ref, kp_ref, kc_ref, kn_ref, vp_ref, vc_ref, vn_ref, o_ref, lse_ref,
                 kbuf, vbuf, *, tq, half, sub_len):
    i = pl.program_id(2)
    kbuf[0:half] = kp_ref[...]
    kbuf[half:half + tq] = kc_ref[...]
    kbuf[half + tq:] = kn_ref[...]
    vbuf[0:half] = vp_ref[...]
    vbuf[half:half + tq] = vc_ref[...]
    vbuf[half + tq:] = vn_ref[...]

    sub = 2 * half
    win = sub + 2 * half
    row = lax.broadcasted_iota(jnp.int32, (sub, win), 0)
    col = lax.broadcasted_iota(jnp.int32, (sub, win), 1)
    band = (col - row).astype(jnp.uint32) <= jnp.uint32(2 * half)
    lane = lax.broadcasted_iota(jnp.int32, (sub, LANES), 1)
    for s in range(tq // sub):
        kpos = i * tq + (s * sub - half) + col
        inside = kpos.astype(jnp.uint32) < jnp.uint32(sub_len)
        bias = jnp.where(band, jnp.where(inside, 0.0, NEG_INF), NEG_INF).astype(F32)
        lse_tile = jnp.zeros((sub, LANES), F32)
        for j in range(HEADS_PER_GROUP):
            hs = slice(j * HEAD_DIM, (j + 1) * HEAD_DIM)
            q = q_ref[s * sub:(s + 1) * sub, hs]
            k = kbuf[s * sub:s * sub + win, hs]
            v = vbuf[s * sub:s * sub + win, hs]
            sc = lax.dot_general(q, k, (((1,), (1,)), ((), ())), preferred_element_type=F32)
            sc = sc + bias
            m = jnp.max(sc, axis=-1, keepdims=True)
            e = jnp.exp(sc - m)
            den = jnp.sum(e, axis=-1, keepdims=True)
            o = jnp.dot(e.astype(BF16), v, preferred_element_type=F32) / den
            o_ref[s * sub:(s + 1) * sub, hs] = o.astype(BF16)
            lse_tile = jnp.where(lane == j, m + jnp.log(den), lse_tile)
        lse_ref[s * sub:(s + 1) * sub, :] = lse_tile


def _attention(qkv_g, window, dil):
    n_seq, _, L, _ = qkv_g.shape
    half = window // (2 * dil)
    tq = min(ATTN_ROWS, L)
    assert L % tq == 0 and tq % (2 * half) == 0 and half % 16 == 0
    per = tq // half
    last = L // half - 1

    def cur(c):
        return pl.BlockSpec((None, None, tq, GROUP_WIDTH), lambda b, r, i: (b, r, i, c))

    def prev(c):
        return pl.BlockSpec((None, None, half, GROUP_WIDTH),
                            lambda b, r, i: (b, r, jnp.maximum(i * per - 1, 0), c))

    def nxt(c):
        return pl.BlockSpec((None, None, half, GROUP_WIDTH),
                            lambda b, r, i: (b, r, jnp.minimum((i + 1) * per, last), c))

    kernel = functools.partial(_attn_kernel, tq=tq, half=half, sub_len=L)
    return pl.pallas_call(
        kernel,
        out_shape=[jax.ShapeDtypeStruct((n_seq, dil, L, GROUP_WIDTH), BF16),
                   jax.ShapeDtypeStruct((n_seq, dil, L, LANES), F32)],
        grid=(n_seq, dil, L // tq),
        in_specs=[cur(0), prev(1), cur(1), nxt(1), prev(2), cur(2), nxt(2)],
        out_specs=[pl.BlockSpec((None, None, tq, GROUP_WIDTH), lambda b, r, i: (b, r, i, 0)),
                   pl.BlockSpec((None, None, tq, LANES), lambda b, r, i: (b, r, i, 0))],
        scratch_shapes=[pltpu.VMEM((tq + 2 * half, GROUP_WIDTH), BF16),
                        pltpu.VMEM((tq + 2 * half, GROUP_WIDTH), BF16)],
        compiler_params=_params(3),
        name=f"band_attn_d{dil}",
    )(qkv_g, qkv_g, qkv_g, qkv_g, qkv_g, qkv_g, qkv_g)


def _wo_kernel(*refs, tm, bounds):
    n_x = len(bounds)
    x_refs = refs[:n_x]
    o_refs = refs[n_x:n_x + N_GROUPS]
    l_refs = refs[n_x + N_GROUPS:n_x + 2 * N_GROUPS]
    w_ref, out_ref, obuf, lbuf = refs[n_x + 2 * N_GROUPS:]
    for g, (_, dil) in enumerate(DILATION_GROUPS):
        for r in range(dil):
            rows = pl.ds(r, tm // dil, stride=dil) if dil > 1 else slice(None)
            for j in range(HEADS_PER_GROUP):
                hs = slice(j * HEAD_DIM, (j + 1) * HEAD_DIM)
                obuf[g * HEADS_PER_GROUP + j, rows, :] = o_refs[g][r, :, hs].astype(F32)
            lbuf[g, rows, :] = l_refs[g][r]
    lse = [lbuf[g] for g in range(N_GROUPS)]
    m = jnp.maximum(jnp.maximum(lse[0], lse[1]), lse[2])
    e = [jnp.exp(l - m) for l in lse]
    inv = 1.0 / (e[0] + e[1] + e[2])
    acc = _current_part(x_refs, bounds)
    for g in range(N_GROUPS):
        lam = e[g] * inv
        heads = [obuf[g * HEADS_PER_GROUP + j] * lam[:, j:j + 1]
                 for j in range(HEADS_PER_GROUP)]
        a = jnp.concatenate(heads, axis=-1).astype(BF16)
        acc = acc + jnp.dot(a, w_ref[g * GROUP_WIDTH:(g + 1) * GROUP_WIDTH, :],
                            preferred_element_type=F32)
    out_ref[...] = acc


def _wo(parts, outs, lses, w_o, n_seq, seq_len):
    D = parts[0].shape[-1]
    T = n_seq * seq_len
    tm = min(MIX_ROWS, seq_len)
    x_specs, x_arrays, bounds = _part_specs(parts, seq_len, tm, D)
    o_specs, l_specs = [], []
    for _, dil in DILATION_GROUPS:
        o_specs.append(pl.BlockSpec((None, dil, tm // dil, GROUP_WIDTH), lambda b, i: (b, 0, i, 0)))
        l_specs.append(pl.BlockSpec((None, dil, tm // dil, LANES), lambda b, i: (b, 0, i, 0)))
    out = pl.pallas_call(
        functools.partial(_wo_kernel, tm=tm, bounds=bounds),
        out_shape=jax.ShapeDtypeStruct((n_seq, seq_len, D), F32),
        grid=(n_seq, seq_len // tm),
        in_specs=[*x_specs, *o_specs, *l_specs,
                  _resident((ATTN_WIDTH, D), lambda b, i: (0, 0))],
        out_specs=pl.BlockSpec((None, tm, D), lambda b, i: (b, i, 0)),
        scratch_shapes=[pltpu.VMEM((N_GROUPS * HEADS_PER_GROUP, tm, HEAD_DIM), F32),
                        pltpu.VMEM((N_GROUPS, tm, LANES), F32)],
        compiler_params=_params(2),
        name="attn_out_proj",
    )(*x_arrays, *outs, *lses, w_o)
    return out.reshape(T, D)


def _glu_kernel(x_ref, g_ref, w_ref, b_ref, u_ref, xn_ref, *, n_ch, tn):
    xn_ref[...] = _rms(x_ref[...], g_ref[...]).astype(BF16)
    xn = xn_ref[...]
    for c in range(0, n_ch, tn):
        a = jnp.dot(xn, w_ref[:, c:c + tn], preferred_element_type=F32) + b_ref[:, c:c + tn]
        gt = (jnp.dot(xn, w_ref[:, n_ch + c:n_ch + c + tn], preferred_element_type=F32)
              + b_ref[:, n_ch + c:n_ch + c + tn])
        u_ref[:, c:c + tn] = a * _sigmoid(gt)


def _glu(x, norm_g, w_in, b_in):
    T, D = x.shape
    n_ch = w_in.shape[-1] // 2
    tm = min(MIX_ROWS, T)
    tn = min(512, n_ch)
    assert T % tm == 0 and n_ch % tn == 0
    return pl.pallas_call(
        functools.partial(_glu_kernel, n_ch=n_ch, tn=tn),
        out_shape=jax.ShapeDtypeStruct((T, n_ch), F32),
        grid=(T // tm,),
        in_specs=[pl.BlockSpec((tm, D), lambda i: (i, 0)),
                  pl.BlockSpec((1, D), lambda i: (0, 0)),
                  _resident((D, 2 * n_ch), lambda i: (0, 0)),
                  pl.BlockSpec((1, 2 * n_ch), lambda i: (0, 0))],
        out_specs=pl.BlockSpec((tm, n_ch), lambda i: (i, 0)),
        scratch_shapes=[pltpu.VMEM((tm, D), BF16)],
        compiler_params=_params(1),
        name="conv_in_glu",
    )(x, norm_g.reshape(1, D), w_in, b_in.reshape(1, 2 * n_ch))


def _conv_out_kernel(x_ref, up_ref, uc_ref, un_ref, wdw_ref, bdw_ref, lng_ref, lnb_ref,
                     w_ref, bo_ref, out_ref, ubuf, cbuf, hbuf, *, tm, n_ch):
    i = pl.program_id(1)
    n_chunks = n_ch // LANES
    first = i == 0
    last = i == pl.num_programs(1) - 1
    for c in range(n_chunks):
        cs = slice(c * LANES, (c + 1) * LANES)
        ubuf[c, 0:CONV_HALO, :] = jnp.where(first, 0.0, up_ref[:, cs])
        ubuf[c, CONV_HALO:CONV_HALO + tm, :] = uc_ref[:, cs]
        ubuf[c, CONV_HALO + tm:, :] = jnp.where(last, 0.0, un_ref[:, cs])

    def conv_chunk(c, carry):
        w = wdw_ref[c]
        b = bdw_ref[c]
        for r0 in range(0, tm, CONV_CHUNK):
            acc = jnp.zeros((CONV_CHUNK, LANES), F32)
            for j in range(CONV_WIDTH):
                start = r0 + CONV_HALO - CONV_PAD + j
                acc = acc + ubuf[c, start:start + CONV_CHUNK, :] * w[j:j + 1, :]
            cbuf[c, r0:r0 + CONV_CHUNK, :] = acc + b
        return carry

    lax.fori_loop(0, n_chunks, conv_chunk, 0)

    tot = cbuf[0]
    for c in range(1, n_chunks):
        tot = tot + cbuf[c]
    mu = jnp.sum(tot, axis=-1, keepdims=True) * (1.0 / n_ch)
    sq = jnp.zeros((tm, LANES), F32)
    for c in range(n_chunks):
        d = cbuf[c] - mu
        sq = sq + d * d
    var = jnp.sum(sq, axis=-1, keepdims=True) * (1.0 / n_ch)
    rstd = lax.rsqrt(var + LN_EPS)
    for c in range(n_chunks):
        cs = slice(c * LANES, (c + 1) * LANES)
        y = (cbuf[c] - mu) * rstd * lng_ref[:, cs] + lnb_ref[:, cs]
        hbuf[:, cs] = (y * _sigmoid(y)).astype(BF16)

    out_ref[...] = (x_ref[...] + bo_ref[...]
                    + jnp.dot(hbuf[...], w_ref[...], preferred_element_type=F32))


def _conv_out(x, u, w_dw, b_dw, ln_g, ln_b, w_out, b_out, n_seq, seq_len):
    T, D = x.shape
    n_ch = u.shape[-1]
    tm = min(MIX_ROWS, seq_len)
    assert seq_len % tm == 0 and tm % CONV_HALO == 0 and tm % CONV_CHUNK == 0
    assert n_ch % LANES == 0 and CONV_HALO >= CONV_PAD
    n_chunks = n_ch // LANES
    per = tm // CONV_HALO
    last = seq_len // CONV_HALO - 1
    x3 = x.reshape(n_seq, seq_len, D)
    u3 = u.reshape(n_seq, seq_len, n_ch)
    w_dw_c = w_dw.reshape(CONV_WIDTH, n_chunks, LANES).transpose(1, 0, 2)
    b_dw_c = b_dw.reshape(n_chunks, 1, LANES)
    out = pl.pallas_call(
        functools.partial(_conv_out_kernel, tm=tm, n_ch=n_ch),
        out_shape=jax.ShapeDtypeStruct((n_seq, seq_len, D), F32),
        grid=(n_seq, seq_len // tm),
        in_specs=[
            pl.BlockSpec((None, tm, D), lambda b, i: (b, i, 0)),
            pl.BlockSpec((None, CONV_HALO, n_ch),
                         lambda b, i: (b, jnp.maximum(i * per - 1, 0), 0)),
            pl.BlockSpec((None, tm, n_ch), lambda b, i: (b, i, 0)),
            pl.BlockSpec((None, CONV_HALO, n_ch),
                         lambda b, i: (b, jnp.minimum((i + 1) * per, last), 0)),
            pl.BlockSpec((n_chunks, CONV_WIDTH, LANES), lambda b, i: (0, 0, 0)),
            pl.BlockSpec((n_chunks, 1, LANES), lambda b, i: (0, 0, 0)),
            pl.BlockSpec((1, n_ch), lambda b, i: (0, 0)),
            pl.BlockSpec((1, n_ch), lambda b, i: (0, 0)),
            _resident((n_ch, D), lambda b, i: (0, 0)),
            pl.BlockSpec((1, D), lambda b, i: (0, 0)),
        ],
        out_specs=pl.BlockSpec((None, tm, D), lambda b, i: (b, i, 0)),
        scratch_shapes=[pltpu.VMEM((n_chunks, tm + 2 * CONV_HALO, LANES), F32),
                        pltpu.VMEM((n_chunks, tm, LANES), F32),
                        pltpu.VMEM((tm, n_ch), BF16)],
        compiler_params=_params(2),
        name="conv_ln_out",
    )(x3, u3, u3, u3, w_dw_c, b_dw_c, ln_g.reshape(1, n_ch), ln_b.reshape(1, n_ch),
      w_out, b_out.reshape(1, D))
    return out.reshape(T, D)


def kernel(x_prompt, x_sample, ffn_norm, ffn_w_gate, ffn_w_up, ffn_w_down, mix_norm, attn_w_qkv, attn_w_o, conv_w_in, conv_b_in, conv_w_dw, conv_b_dw, conv_ln_g, conv_ln_b, conv_w_out, conv_b_out, final_norm):
    n_prompt, seq_len, D = x_prompt.shape
    n_sample = x_sample.shape[0]
    assert x_sample.shape[1:] == (seq_len, D)
    n_seq = n_prompt + n_sample
    depth = ffn_norm.shape[0]

    wg, wu, wd = (w.astype(BF16) for w in (ffn_w_gate, ffn_w_up, ffn_w_down))
    w_qkv, w_o = attn_w_qkv.astype(BF16), attn_w_o.astype(BF16)
    w_in, w_out = conv_w_in.astype(BF16), conv_w_out.astype(BF16)
    tables = _rope_tables(seq_len)

    n_p = n_prompt * seq_len
    n_s = n_sample * seq_len
    ffn_rows = FFN_ROWS
    while n_p % ffn_rows or n_s % ffn_rows:
        ffn_rows //= 2

    parts = [x_prompt.reshape(-1, D), x_sample.reshape(-1, D)]
    for i in range(depth):
        parts = [_ffn(x, ffn_norm, wg, wu, wd, i, 0, ffn_rows) for x in parts]
        j = i // 2
        if i % 2 == 0:
            qkv = _qkv(parts, mix_norm[i], w_qkv[j], tables, n_seq, seq_len)
            outs, lses = [], []
            for g, (window, dil) in enumerate(DILATION_GROUPS):
                o, l = _attention(qkv[g], window, dil)
                outs.append(o)
                lses.append(l)
            x = _wo(parts, outs, lses, w_o[j], n_seq, seq_len)
        else:
            x, = parts
            u = _glu(x, mix_norm[i], w_in[j], conv_b_in[j])
            x = _conv_out(x, u, conv_w_dw[j], conv_b_dw[j], conv_ln_g[j], conv_ln_b[j],
                          w_out[j], conv_b_out[j], n_seq, seq_len)
        if i < depth - 1:
            parts = [_ffn(x, ffn_norm, wg, wu, wd, i, 1, ffn_rows)]
    last = functools.partial(_ffn, x, ffn_norm, wg, wu, wd, depth - 1, 1, ffn_rows,
                             final_g=final_norm)
    y_prompt = last(row0=0, rows=n_p)
    y_sample = last(row0=n_p, rows=n_s)
    return (y_prompt.reshape(n_prompt, seq_len, D), y_sample.reshape(n_sample, seq_len, D))
```

```python
import functools

import jax
import jax.numpy as jnp
from jax import lax
from jax.experimental import pallas as pl
from jax.experimental.pallas import tpu as pltpu

F32 = jnp.float32
BF16 = jnp.bfloat16

HEAD_DIM = 128
ROT_DIM = HEAD_DIM // 4
ROPE_THETA = 500000.0
DILATION_GROUPS = ((128, 1), (512, 4), (2048, 16))
HEADS_PER_GROUP = 4
GROUP_WIDTH = HEADS_PER_GROUP * HEAD_DIM
N_GROUPS = len(DILATION_GROUPS)
ATTN_WIDTH = N_GROUPS * GROUP_WIDTH
CONV_WIDTH = 31
CONV_PAD = CONV_WIDTH // 2
FFN_RESIDUAL_WEIGHT = 0.5
RMS_EPS = 1e-6
LN_EPS = 1e-5
NEG_INF = -1e30

LANES = 128
VMEM_LIMIT_BYTES = 60 * 1024 * 1024

FFN_ROWS = 1024
FFN_COLS = 512
MIX_ROWS = 512
GLU_ROWS = 1024
ATTN_ROWS = 2048
CONV_HALO = 16
CONV_CHUNK = 64


def _params(n_grid):
    return pltpu.CompilerParams(
        dimension_semantics=("arbitrary",) * n_grid,
        vmem_limit_bytes=VMEM_LIMIT_BYTES)


def _resident(shape, index_map):
    return pl.BlockSpec(shape, index_map, pipeline_mode=pl.Buffered(1))


def _rms(x, g):
    y = x * lax.rsqrt(jnp.mean(x * x, axis=-1, keepdims=True) + RMS_EPS)
    return y * g


def _sigmoid(x):
    return 1.0 / (1.0 + jnp.exp(-x))


def _ffn_kernel(x_ref, g_ref, wg_ref, wu_ref, wd_ref, fg_ref, o_ref, xn_ref, *, final_norm):
    f = pl.program_id(1)

    @pl.when(f == 0)
    def _():
        x = x_ref[...]
        xn_ref[...] = _rms(x, g_ref[...]).astype(BF16)
        o_ref[...] = x

    xn = xn_ref[...]
    gate = jnp.dot(xn, wg_ref[...], preferred_element_type=F32)
    up = jnp.dot(xn, wu_ref[...], preferred_element_type=F32)
    h = (gate * _sigmoid(gate) * up * FFN_RESIDUAL_WEIGHT).astype(BF16)
    o_ref[...] += jnp.dot(h, wd_ref[...], preferred_element_type=F32)

    if final_norm:
        @pl.when(f == pl.num_programs(1) - 1)
        def _():
            o_ref[...] = _rms(o_ref[...], fg_ref[...])


def _ffn(x, norm_g, w_gate, w_up, w_down, layer, which, tm, row0=0, rows=None, final_g=None):
    T, D = x.shape
    F = w_gate.shape[-1]
    rows = T if rows is None else rows
    tf = min(FFN_COLS, F)
    assert rows % tm == 0 and row0 % tm == 0 and F % tf == 0
    tile0 = row0 // tm
    final_norm = final_g is not None
    if final_g is None:
        final_g = norm_g[layer, which]
    kernel = functools.partial(_ffn_kernel, final_norm=final_norm)
    return pl.pallas_call(
        kernel,
        out_shape=jax.ShapeDtypeStruct((rows, D), F32),
        grid=(rows // tm, F // tf),
        in_specs=[
            pl.BlockSpec((tm, D), lambda i, f: (tile0 + i, 0)),
            pl.BlockSpec((None, None, 1, D), lambda i, f: (layer, which, 0, 0)),
            pl.BlockSpec((None, None, D, tf), lambda i, f: (layer, which, 0, f)),
            pl.BlockSpec((None, None, D, tf), lambda i, f: (layer, which, 0, f)),
            pl.BlockSpec((None, None, tf, D), lambda i, f: (layer, which, f, 0)),
            pl.BlockSpec((1, D), lambda i, f: (0, 0)),
        ],
        out_specs=pl.BlockSpec((tm, D), lambda i, f: (i, 0)),
        scratch_shapes=[pltpu.VMEM((tm, D), BF16)],
        compiler_params=_params(2),
        name="ffn",
    )(x, norm_g.reshape(norm_g.shape[0], 2, 1, D), w_gate, w_up, w_down,
      final_g.reshape(1, D))


def _rope_tables(seq_len):
    half = ROT_DIM // 2
    pos = jnp.arange(seq_len, dtype=F32)
    freqs = ROPE_THETA ** (-jnp.arange(0, ROT_DIM, 2, dtype=F32) / ROT_DIM)
    ang = pos[:, None] * freqs[None, :]
    cos, sin = jnp.cos(ang), jnp.sin(ang)
    zeros = jnp.zeros((seq_len, HEAD_DIM - ROT_DIM), F32)
    zhalf = jnp.zeros((seq_len, half), F32)
    c = jnp.concatenate([cos, cos, jnp.ones_like(zeros)], axis=-1)
    s_lo = jnp.concatenate([-sin, zhalf, zeros], axis=-1)
    s_hi = jnp.concatenate([zhalf, sin, zeros], axis=-1)
    return c, s_lo, s_hi


def _part_specs(parts, seq_len, tm, D):
    specs, arrays, bounds = [], [], []
    start = 0
    for arr in parts:
        n = arr.shape[0] // seq_len
        last_tile = seq_len // tm - 1

        def index_map(b, i, start=start, n=n):
            tile = jnp.where(b < start, 0, jnp.where(b >= start + n, last_tile, i))
            return (jnp.clip(b - start, 0, n - 1), tile, 0)

        specs.append(pl.BlockSpec((None, tm, D), index_map))
        arrays.append(arr.reshape(n, seq_len, D))
        bounds.append((start, start + n))
        start += n
    return specs, arrays, bounds


def _current_part(x_refs, bounds):
    b = pl.program_id(0)
    x = x_refs[-1][...]
    for x_ref, (_, hi) in reversed(list(zip(x_refs[:-1], bounds[:-1]))):
        x = jnp.where(b < hi, x_ref[...], x)
    return x


def _qkv_kernel(*refs, tm, bounds):
    n_x = len(bounds)
    x_refs = refs[:n_x]
    (g_ref, w_ref, c_ref, slo_ref, shi_ref, o0_ref, o1_ref, o2_ref, xn_ref, t_ref) = refs[n_x:]

    xn_ref[...] = _rms(_current_part(x_refs, bounds), g_ref[...]).astype(BF16)
    xn = xn_ref[...]
    c, s_lo, s_hi = c_ref[...], slo_ref[...], shi_ref[...]
    half = ROT_DIM // 2
    scale = HEAD_DIM ** -0.5
    o_refs = (o0_ref, o1_ref, o2_ref)
    for part in range(3):
        for g, (_, dil) in reversed(list(enumerate(DILATION_GROUPS))):
            col = part * ATTN_WIDTH + g * GROUP_WIDTH
            acc = jnp.dot(xn, w_ref[:, col:col + GROUP_WIDTH], preferred_element_type=F32)
            dst = o_refs[g]
            for j in range(HEADS_PER_GROUP):
                t = acc[:, j * HEAD_DIM:(j + 1) * HEAD_DIM]
                if part < 2:
                    t = (t * c + pltpu.roll(t, HEAD_DIM - half, 1) * s_lo
                         + pltpu.roll(t, half, 1) * s_hi)
                if part == 0:
                    t = t * scale
                cols = slice(part * GROUP_WIDTH + j * HEAD_DIM,
                             part * GROUP_WIDTH + (j + 1) * HEAD_DIM)
                if dil == 1:
                    dst[0, :, cols] = t.astype(BF16)
                else:
                    t_ref[j] = t
                    for r in range(dil):
                        dst[r, :, cols] = t_ref[j, pl.ds(r, tm // dil, stride=dil), :].astype(BF16)


def _qkv(parts, norm_g, w_qkv, tables, n_seq, seq_len):
    D = parts[0].shape[-1]
    tm = min(MIX_ROWS, seq_len)
    assert seq_len % tm == 0
    W = w_qkv.shape[-1]
    x_specs, x_arrays, bounds = _part_specs(parts, seq_len, tm, D)
    out_shapes, out_specs = [], []
    for _, dil in DILATION_GROUPS:
        assert tm % (dil * 16) == 0
        out_shapes.append(jax.ShapeDtypeStruct((n_seq, dil, seq_len // dil, 3 * GROUP_WIDTH), BF16))
        out_specs.append(pl.BlockSpec((None, dil, tm // dil, 3 * GROUP_WIDTH),
                                      lambda b, i: (b, 0, i, 0)))
    tab_spec = pl.BlockSpec((tm, HEAD_DIM), lambda b, i: (i, 0))
    return pl.pallas_call(
        functools.partial(_qkv_kernel, tm=tm, bounds=bounds),
        out_shape=out_shapes,
        grid=(n_seq, seq_len // tm),
        in_specs=[
            *x_specs,
            pl.BlockSpec((1, D), lambda b, i: (0, 0)),
            _resident((D, W), lambda b, i: (0, 0)),
            tab_spec, tab_spec, tab_spec,
        ],
        out_specs=out_specs,
        scratch_shapes=[pltpu.VMEM((tm, D), BF16),
                        pltpu.VMEM((HEADS_PER_GROUP, tm, HEAD_DIM), F32)],
        compiler_params=_params(2),
        name="qkv_rope",
    )(*x_arrays, norm_g.reshape(1, D), w_qkv, *tables)


def _attn_kernel(q_ref, kp_ref, kc_ref, kn_ref, vp_ref, vc_ref, vn_ref, o_ref, lse_ref,
                 kbuf, vbuf, *, tq, half, sub_len):
    i = pl.program_id(2)
    kbuf[0:half] = kp_ref[...]
    kbuf[half:half + tq] = kc_ref[...]
    kbuf[half + tq:] = kn_ref[...]
    vbuf[0:half] = vp_ref[...]
    vbuf[half:half + tq] = vc_ref[...]
    vbuf[half + tq:] = vn_ref[...]

    sub = 2 * half
    win = sub + 2 * half
    row = lax.broadcasted_iota(jnp.int32, (sub, win), 0)
    col = lax.broadcasted_iota(jnp.int32, (sub, win), 1)
    band = (col - row).astype(jnp.uint32) <= jnp.uint32(2 * half)
    lane = lax.broadcasted_iota(jnp.int32, (sub, LANES), 1)
    for s in range(tq // sub):
        kpos = i * tq + (s * sub - half) + col
        inside = kpos.astype(jnp.uint32) < jnp.uint32(sub_len)
        bias = jnp.where(band, jnp.where(inside, 0.0, NEG_INF), NEG_INF).astype(F32)
        lse_tile = jnp.zeros((sub, LANES), F32)
        for j in range(HEADS_PER_GROUP):
            hs = slice(j * HEAD_DIM, (j + 1) * HEAD_DIM)
            q = q_ref[s * sub:(s + 1) * sub, hs]
            k = kbuf[s * sub:s * sub + win, hs]
            v = vbuf[s * sub:s * sub + win, hs]
            sc = lax.dot_general(q, k, (((1,), (1,)), ((), ())), preferred_element_type=F32)
            sc = sc + bias
            m = jnp.max(sc, axis=-1, keepdims=True)
            e = jnp.exp(sc - m)
            den = jnp.sum(e, axis=-1, keepdims=True)
            o = jnp.dot(e.astype(BF16), v, preferred_element_type=F32) / den
            o_ref[s * sub:(s + 1) * sub, hs] = o.astype(BF16)
            lse_tile = jnp.where(lane == j, m + jnp.log(den), lse_tile)
        lse_ref[s * sub:(s + 1) * sub, :] = lse_tile


def _attention(qkv_g, window, dil):
    n_seq, _, L, _ = qkv_g.shape
    half = window // (2 * dil)
    tq = min(ATTN_ROWS, L)
    assert L % tq == 0 and tq % (2 * half) == 0 and half % 16 == 0
    per = tq // half
    last = L // half - 1

    def cur(c):
        return pl.BlockSpec((None, None, tq, GROUP_WIDTH), lambda b, r, i: (b, r, i, c))

    def prev(c):
        return pl.BlockSpec((None, None, half, GROUP_WIDTH),
                            lambda b, r, i: (b, r, jnp.maximum(i * per - 1, 0), c))

    def nxt(c):
        return pl.BlockSpec((None, None, half, GROUP_WIDTH),
                            lambda b, r, i: (b, r, jnp.minimum((i + 1) * per, last), c))

    kernel = functools.partial(_attn_kernel, tq=tq, half=half, sub_len=L)
    return pl.pallas_call(
        kernel,
        out_shape=[jax.ShapeDtypeStruct((n_seq, dil, L, GROUP_WIDTH), BF16),
                   jax.ShapeDtypeStruct((n_seq, dil, L, LANES), F32)],
        grid=(n_seq, dil, L // tq),
        in_specs=[cur(0), prev(1), cur(1), nxt(1), prev(2), cur(2), nxt(2)],
        out_specs=[pl.BlockSpec((None, None, tq, GROUP_WIDTH), lambda b, r, i: (b, r, i, 0)),
                   pl.BlockSpec((None, None, tq, LANES), lambda b, r, i: (b, r, i, 0))],
        scratch_shapes=[pltpu.VMEM((tq + 2 * half, GROUP_WIDTH), BF16),
                        pltpu.VMEM((tq + 2 * half, GROUP_WIDTH), BF16)],
        compiler_params=_params(3),
        name=f"band_attn_d{dil}",
    )(qkv_g, qkv_g, qkv_g, qkv_g, qkv_g, qkv_g, qkv_g)


def _wo_kernel(*refs, tm, bounds):
    n_x = len(bounds)
    x_refs = refs[:n_x]
    o_refs = refs[n_x:n_x + N_GROUPS]
    l_refs = refs[n_x + N_GROUPS:n_x + 2 * N_GROUPS]
    w_ref, out_ref, obuf, lbuf = refs[n_x + 2 * N_GROUPS:]
    for g, (_, dil) in enumerate(DILATION_GROUPS):
        for r in range(dil):
            rows = pl.ds(r, tm // dil, stride=dil) if dil > 1 else slice(None)
            for j in range(HEADS_PER_GROUP):
                hs = slice(j * HEAD_DIM, (j + 1) * HEAD_DIM)
                obuf[g * HEADS_PER_GROUP + j, rows, :] = o_refs[g][r, :, hs].astype(F32)
            lbuf[g, rows, :] = l_refs[g][r]
    lse = [lbuf[g] for g in range(N_GROUPS)]
    m = jnp.maximum(jnp.maximum(lse[0], lse[1]), lse[2])
    e = [jnp.exp(l - m) for l in lse]
    inv = 1.0 / (e[0] + e[1] + e[2])
    acc = _current_part(x_refs, bounds)
    for g in range(N_GROUPS):
        lam = e[g] * inv
        heads = [obuf[g * HEADS_PER_GROUP + j] * lam[:, j:j + 1]
                 for j in range(HEADS_PER_GROUP)]
        a = jnp.concatenate(heads, axis=-1).astype(BF16)
        acc = acc + jnp.dot(a, w_ref[g * GROUP_WIDTH:(g + 1) * GROUP_WIDTH, :],
                            preferred_element_type=F32)
    out_ref[...] = acc


def _wo(parts, outs, lses, w_o, n_seq, seq_len):
    D = parts[0].shape[-1]
    T = n_seq * seq_len
    tm = min(MIX_ROWS, seq_len)
    x_specs, x_arrays, bounds = _part_specs(parts, seq_len, tm, D)
    o_specs, l_specs = [], []
    for _, dil in DILATION_GROUPS:
        o_specs.append(pl.BlockSpec((None, dil, tm // dil, GROUP_WIDTH), lambda b, i: (b, 0, i, 0)))
        l_specs.append(pl.BlockSpec((None, dil, tm // dil, LANES), lambda b, i: (b, 0, i, 0)))
    out = pl.pallas_call(
        functools.partial(_wo_kernel, tm=tm, bounds=bounds),
        out_shape=jax.ShapeDtypeStruct((n_seq, seq_len, D), F32),
        grid=(n_seq, seq_len // tm),
        in_specs=[*x_specs, *o_specs, *l_specs,
                  _resident((ATTN_WIDTH, D), lambda b, i: (0, 0))],
        out_specs=pl.BlockSpec((None, tm, D), lambda b, i: (b, i, 0)),
        scratch_shapes=[pltpu.VMEM((N_GROUPS * HEADS_PER_GROUP, tm, HEAD_DIM), F32),
                        pltpu.VMEM((N_GROUPS, tm, LANES), F32)],
        compiler_params=_params(2),
        name="attn_out_proj",
    )(*x_arrays, *outs, *lses, w_o)
    return out.reshape(T, D)


def _glu_kernel(x_ref, g_ref, w_ref, b_ref, u_ref, xn_ref, *, n_ch, tn):
    xn_ref[...] = _rms(x_ref[...], g_ref[...]).astype(BF16)
    xn = xn_ref[...]
    for c in range(0, n_ch, tn):
        a = jnp.dot(xn, w_ref[:, c:c + tn], preferred_element_type=F32) + b_ref[:, c:c + tn]
        gt = (jnp.dot(xn, w_ref[:, n_ch + c:n_ch + c + tn], preferred_element_type=F32)
              + b_ref[:, n_ch + c:n_ch + c + tn])
        u_ref[:, c:c + tn] = a * _sigmoid(gt)


def _glu(x, norm_g, w_in, b_in):
    T, D = x.shape
    n_ch = w_in.shape[-1] // 2
    tm = min(GLU_ROWS, T)
    tn = min(512, n_ch)
    assert T % tm == 0 and n_ch % tn == 0
    return pl.pallas_call(
        functools.partial(_glu_kernel, n_ch=n_ch, tn=tn),
        out_shape=jax.ShapeDtypeStruct((T, n_ch), F32),
        grid=(T // tm,),
        in_specs=[pl.BlockSpec((tm, D), lambda i: (i, 0)),
                  pl.BlockSpec((1, D), lambda i: (0, 0)),
                  _resident((D, 2 * n_ch), lambda i: (0, 0)),
                  pl.BlockSpec((1, 2 * n_ch), lambda i: (0, 0))],
        out_specs=pl.BlockSpec((tm, n_ch), lambda i: (i, 0)),
        scratch_shapes=[pltpu.VMEM((tm, D), BF16)],
        compiler_params=_params(1),
        name="conv_in_glu",
    )(x, norm_g.reshape(1, D), w_in, b_in.reshape(1, 2 * n_ch))


def _conv_out_kernel(x_ref, up_ref, uc_ref, un_ref, wdw_ref, bdw_ref, lng_ref, lnb_ref,
                     w_ref, bo_ref, out_ref, ubuf, cbuf, hbuf, *, tm, n_ch):
    i = pl.program_id(1)
    n_chunks = n_ch // LANES
    first = i == 0
    last = i == pl.num_programs(1) - 1
    for c in range(n_chunks):
        cs = slice(c * LANES, (c + 1) * LANES)
        ubuf[c, 0:CONV_HALO, :] = jnp.where(first, 0.0, up_ref[:, cs])
        ubuf[c, CONV_HALO:CONV_HALO + tm, :] = uc_ref[:, cs]
        ubuf[c, CONV_HALO + tm:, :] = jnp.where(last, 0.0, un_ref[:, cs])

    def conv_chunk(c, carry):
        w = wdw_ref[c]
        b = bdw_ref[c]
        for r0 in range(0, tm, CONV_CHUNK):
            acc = jnp.zeros((CONV_CHUNK, LANES), F32)
            for j in range(CONV_WIDTH):
                start = r0 + CONV_HALO - CONV_PAD + j
                acc = acc + ubuf[c, start:start + CONV_CHUNK, :] * w[j:j + 1, :]
            cbuf[c, r0:r0 + CONV_CHUNK, :] = acc + b
        return carry

    lax.fori_loop(0, n_chunks, conv_chunk, 0)

    tot = cbuf[0]
    for c in range(1, n_chunks):
        tot = tot + cbuf[c]
    mu = jnp.sum(tot, axis=-1, keepdims=True) * (1.0 / n_ch)
    sq = jnp.zeros((tm, LANES), F32)
    for c in range(n_chunks):
        d = cbuf[c] - mu
        sq = sq + d * d
    var = jnp.sum(sq, axis=-1, keepdims=True) * (1.0 / n_ch)
    rstd = lax.rsqrt(var + LN_EPS)
    for c in range(n_chunks):
        cs = slice(c * LANES, (c + 1) * LANES)
        y = (cbuf[c] - mu) * rstd * lng_ref[:, cs] + lnb_ref[:, cs]
        hbuf[:, cs] = (y * _sigmoid(y)).astype(BF16)

    out_ref[...] = (x_ref[...] + bo_ref[...]
                    + jnp.dot(hbuf[...], w_ref[...], preferred_element_type=F32))


def _conv_out(x, u, w_dw, b_dw, ln_g, ln_b, w_out, b_out, n_seq, seq_len):
    T, D = x.shape
    n_ch = u.shape[-1]
    tm = min(MIX_ROWS, seq_len)
    assert seq_len % tm == 0 and tm % CONV_HALO == 0 and tm % CONV_CHUNK == 0
    assert n_ch % LANES == 0 and CONV_HALO >= CONV_PAD
    n_chunks = n_ch // LANES
    per = tm // CONV_HALO
    last = seq_len // CONV_HALO - 1
    x3 = x.reshape(n_seq, seq_len, D)
    u3 = u.reshape(n_seq, seq_len, n_ch)
    w_dw_c = w_dw.reshape(CONV_WIDTH, n_chunks, LANES).transpose(1, 0, 2)
    b_dw_c = b_dw.reshape(n_chunks, 1, LANES)
    out = pl.pallas_call(
        functools.partial(_conv_out_kernel, tm=tm, n_ch=n_ch),
        out_shape=jax.ShapeDtypeStruct((n_seq, seq_len, D), F32),
        grid=(n_seq, seq_len // tm),
        in_specs=[
            pl.BlockSpec((None, tm, D), lambda b, i: (b, i, 0)),
            pl.BlockSpec((None, CONV_HALO, n_ch),
                         lambda b, i: (b, jnp.maximum(i * per - 1, 0), 0)),
            pl.BlockSpec((None, tm, n_ch), lambda b, i: (b, i, 0)),
            pl.BlockSpec((None, CONV_HALO, n_ch),
                         lambda b, i: (b, jnp.minimum((i + 1) * per, last), 0)),
            pl.BlockSpec((n_chunks, CONV_WIDTH, LANES), lambda b, i: (0, 0, 0)),
            pl.BlockSpec((n_chunks, 1, LANES), lambda b, i: (0, 0, 0)),
            pl.BlockSpec((1, n_ch), lambda b, i: (0, 0)),
            pl.BlockSpec((1, n_ch), lambda b, i: (0, 0)),
            _resident((n_ch, D), lambda b, i: (0, 0)),
            pl.BlockSpec((1, D), lambda b, i: (0, 0)),
        ],
        out_specs=pl.BlockSpec((None, tm, D), lambda b, i: (b, i, 0)),
        scratch_shapes=[pltpu.VMEM((n_chunks, tm + 2 * CONV_HALO, LANES), F32),
                        pltpu.VMEM((n_chunks, tm, LANES), F32),
                        pltpu.VMEM((tm, n_ch), BF16)],
        compiler_params=_params(2),
        name="conv_ln_out",
    )(x3, u3, u3, u3, w_dw_c, b_dw_c, ln_g.reshape(1, n_ch), ln_b.reshape(1, n_ch),
      w_out, b_out.reshape(1, D))
    return out.reshape(T, D)


def kernel(x_prompt, x_sample, ffn_norm, ffn_w_gate, ffn_w_up, ffn_w_down, mix_norm, attn_w_qkv, attn_w_o, conv_w_in, conv_b_in, conv_w_dw, conv_b_dw, conv_ln_g, conv_ln_b, conv_w_out, conv_b_out, final_norm):
    n_prompt, seq_len, D = x_prompt.shape
    n_sample = x_sample.shape[0]
    assert x_sample.shape[1:] == (seq_len, D)
    n_seq = n_prompt + n_sample
    depth = ffn_norm.shape[0]

    wg, wu, wd = (w.astype(BF16) for w in (ffn_w_gate, ffn_w_up, ffn_w_down))
    w_qkv, w_o = attn_w_qkv.astype(BF16), attn_w_o.astype(BF16)
    w_in, w_out = conv_w_in.astype(BF16), conv_w_out.astype(BF16)
    tables = _rope_tables(seq_len)

    n_p = n_prompt * seq_len
    n_s = n_sample * seq_len
    ffn_rows = FFN_ROWS
    while n_p % ffn_rows or n_s % ffn_rows:
        ffn_rows //= 2

    parts = [x_prompt.reshape(-1, D), x_sample.reshape(-1, D)]
    for i in range(depth):
        parts = [_ffn(x, ffn_norm, wg, wu, wd, i, 0, ffn_rows) for x in parts]
        j = i // 2
        if i % 2 == 0:
            qkv = _qkv(parts, mix_norm[i], w_qkv[j], tables, n_seq, seq_len)
            outs, lses = [], []
            for g, (window, dil) in enumerate(DILATION_GROUPS):
                o, l = _attention(qkv[g], window, dil)
                outs.append(o)
                lses.append(l)
            x = _wo(parts, outs, lses, w_o[j], n_seq, seq_len)
        else:
            x, = parts
            u = _glu(x, mix_norm[i], w_in[j], conv_b_in[j])
            x = _conv_out(x, u, conv_w_dw[j], conv_b_dw[j], conv_ln_g[j], conv_ln_b[j],
                          w_out[j], conv_b_out[j], n_seq, seq_len)
        if i < depth - 1:
            parts = [_ffn(x, ffn_norm, wg, wu, wd, i, 1, ffn_rows)]
    last = functools.partial(_ffn, x, ffn_norm, wg, wu, wd, depth - 1, 1, ffn_rows,
                             final_g=final_norm)
    y_prompt = last(row0=0, rows=n_p)
    y_sample = last(row0=n_p, rows=n_s)
    return (y_prompt.reshape(n_prompt, seq_len, D), y_sample.reshape(n_sample, seq_len, D))
```
